```python
import jax, jax.numpy as jnp
from jax import lax
import numpy as np

D_MODEL = 4096
BATCH = 2
SEQ = 4096
DEPTH = 2

HEAD_DIM = 128
N_HEADS = D_MODEL // HEAD_DIM
SCALE = HEAD_DIM ** -0.5
ROPE_THETA = 500000.0
ROPE_DIM = HEAD_DIM // 4
NORM_EPS = 1e-5
TINY = 1e-30
N_MIXERS = 2
FFN_PERIOD = 2
DILATED_PAIRS = ((128, 1), (512, 4), (2048, 16))
N_GROUPS_A = len(DILATED_PAIRS)
MOBA_BLOCK = 256
MOBA_TOPK = 3
MOBA_Q_CHUNK = 16
D_FF = (7 * D_MODEL) // 2
N_EXPERTS = 8
TOP_K = 2
D_FF_EXPERT = D_MODEL
MAX_POS_OFFSET = 1024

kernel_name = "hybrid_dilated_moba_moe_trunk"


def rms_norm(x, g):
    xf = x.astype(jnp.float32)
    y = xf * lax.rsqrt(jnp.mean(xf * xf, axis=-1, keepdims=True) + NORM_EPS)
    return (y * g.astype(jnp.float32)).astype(x.dtype)


def rope_partial(x, positions):
    half = ROPE_DIM // 2
    inv_freq = jnp.power(ROPE_THETA, -jnp.arange(half, dtype=jnp.float32) / half)
    ang = positions.astype(jnp.float32)[..., None] * inv_freq
    cos = jnp.cos(ang)[:, :, None, :]
    sin = jnp.sin(ang)[:, :, None, :]
    xr = x[..., :ROPE_DIM].astype(jnp.float32)
    x1, x2 = xr[..., :half], xr[..., half:]
    rot = jnp.concatenate([x1 * cos - x2 * sin, x2 * cos + x1 * sin], axis=-1)
    return jnp.concatenate([rot.astype(x.dtype), x[..., ROPE_DIM:]], axis=-1)


def attend(s, v, spec):
    m = jnp.max(s, axis=-1, keepdims=True)
    m = jnp.where(jnp.isfinite(m), m, 0.0)
    p = jnp.exp(s - m)
    den = jnp.sum(p, axis=-1)
    o = jnp.einsum(spec, p, v) / jnp.maximum(den, TINY)[..., None]
    return o, m[..., 0] + jnp.log(den)


def dilated_window_attention(q, k, v, window, dilation):
    B, S, H, D = q.shape
    span = window // dilation
    L = S // dilation
    nblk = -(-L // span)
    Lp = nblk * span

    def strided(t):
        t = t.reshape(B, L, dilation, H, D).transpose(0, 2, 3, 1, 4)
        return jnp.pad(t, ((0, 0), (0, 0), (0, 0), (0, Lp - L), (0, 0)))

    def with_prev(t):
        tb = t.reshape(B, dilation, H, nblk, span, D)
        prev = jnp.pad(tb, ((0, 0), (0, 0), (0, 0), (1, 0), (0, 0), (0, 0)))[:, :, :, :-1]
        return jnp.concatenate([prev, tb], axis=4)

    qb = strided(q).reshape(B, dilation, H, nblk, span, D)
    kb = with_prev(strided(k))
    vb = with_prev(strided(v))
    s = jnp.einsum('brhnqd,brhnkd->brhnqk', qb, kb) * SCALE
    a = jnp.arange(span)[:, None]
    kk = jnp.arange(2 * span)[None, :]
    band = (kk >= a) & (kk <= a + span)
    not_before_start = (jnp.arange(nblk)[:, None, None] > 0) | (kk[None] >= span)
    s = jnp.where(band[None] & not_before_start, s, -jnp.inf)
    o, lse = attend(s, vb, 'brhnqk,brhnkd->brhnqd')
    o = o.reshape(B, dilation, H, Lp, D)[:, :, :, :L].transpose(0, 3, 1, 2, 4).reshape(B, S, H, D)
    lse = lse.reshape(B, dilation, H, Lp)[..., :L].transpose(0, 3, 1, 2).reshape(B, S, H)
    return o, lse


def dilated_mixer(h, positions, w_in, w_out):
    B, S, _ = h.shape
    proj = jnp.einsum('bsd,de->bse', h, w_in).reshape(B, S, N_GROUPS_A, 3, N_HEADS, HEAD_DIM)
    outs, lses = [], []
    for g, (window, dilation) in enumerate(DILATED_PAIRS):
        q = rope_partial(proj[:, :, g, 0], positions).astype(jnp.float32)
        k = rope_partial(proj[:, :, g, 1], positions).astype(jnp.float32)
        v = proj[:, :, g, 2].astype(jnp.float32)
        o, lse = dilated_window_attention(q, k, v, window, dilation)
        outs.append(o)
        lses.append(lse)
    wts = jax.nn.softmax(jnp.stack(lses, axis=0), axis=0)
    o = jnp.einsum('gbsh,gbshd->bshd', wts, jnp.stack(outs, axis=0))
    return jnp.einsum('bse,ed->bsd', o.reshape(B, S, N_HEADS * HEAD_DIM).astype(h.dtype), w_out)


def moba_mixer(h, positions, w_in, w_out):
    B, S, _ = h.shape
    H, D, BLK = N_HEADS, HEAD_DIM, MOBA_BLOCK
    proj = jnp.einsum('bsd,de->bse', h, w_in).reshape(B, S, 3, H, D)
    q = rope_partial(proj[:, :, 0], positions).astype(jnp.float32).transpose(0, 2, 1, 3)
    k = rope_partial(proj[:, :, 1], positions).astype(jnp.float32).transpose(0, 2, 1, 3)
    v = proj[:, :, 2].astype(jnp.float32).transpose(0, 2, 1, 3)
    nb = -(-S // BLK)
    sp = nb * BLK
    padw = ((0, 0), (0, 0), (0, sp - S), (0, 0))
    qb = jnp.pad(q, padw).reshape(B, H, nb, BLK, D)
    kb = jnp.pad(k, padw).reshape(B, H, nb, BLK, D)
    vb = jnp.pad(v, padw).reshape(B, H, nb, BLK, D)
    s_own = jnp.einsum('bhnqd,bhnkd->bhnqk', qb, kb) * SCALE
    causal = jnp.tril(jnp.ones((BLK, BLK), dtype=bool))
    s_own = jnp.where(causal, s_own, -jnp.inf)
    o_own, lse_own = attend(s_own, vb, 'bhnqk,bhnkd->bhnqd')
    o_own = o_own.reshape(B, H, sp, D)[:, :, :S]
    lse_own = lse_own.reshape(B, H, sp)[:, :, :S]
    n_sel = min(MOBA_TOPK, nb - 1)
    if n_sel > 0:
        k_mean = jnp.mean(kb, axis=3)
        gate = jnp.einsum('bhsd,bhnd->bhsn', q, k_mean)
        q_block = jnp.arange(S) // BLK
        fully_past = jnp.arange(nb)[None, :] < q_block[:, None]
        gate = jnp.where(fully_past, gate, -jnp.inf)
        _, sel = lax.top_k(gate, n_sel)
        sel_ok = jnp.arange(n_sel)[None, :] < q_block[:, None]
        n_chunks = S // MOBA_Q_CHUNK

        def to_chunks(t):
            return jnp.moveaxis(t.reshape(B, H, n_chunks, MOBA_Q_CHUNK, *t.shape[3:]), 2, 0)

        gather = jax.vmap(jax.vmap(lambda blocks, ids: blocks[ids]))

        def chunk_attn(args):
            qc, selc, okc = args
            k_sel = gather(kb, selc)
            v_sel = gather(vb, selc)
            s = jnp.einsum('bhqd,bhqnkd->bhqnk', qc, k_sel) * SCALE
            s = jnp.where(okc[None, None, :, :, None], s, -jnp.inf)
            return attend(s.reshape(B, H, MOBA_Q_CHUNK, n_sel * BLK),
                          v_sel.reshape(B, H, MOBA_Q_CHUNK, n_sel * BLK, D),
                          'bhqk,bhqkd->bhqd')

        o_past, lse_past = lax.map(
            chunk_attn, (to_chunks(q), to_chunks(sel), sel_ok.reshape(n_chunks, MOBA_Q_CHUNK, n_sel)))
        o_past = jnp.moveaxis(o_past, 0, 2).reshape(B, H, S, D)
        lse_past = jnp.moveaxis(lse_past, 0, 2).reshape(B, H, S)
        wts = jax.nn.softmax(jnp.stack([lse_own, lse_past], axis=0), axis=0)
        o = wts[0][..., None] * o_own + wts[1][..., None] * o_past
    else:
        o = o_own
    o = o.transpose(0, 2, 1, 3).reshape(B, S, H * D).astype(h.dtype)
    return jnp.einsum('bse,ed->bsd', o, w_out)


def swiglu(t, w_gate, w_up, w_down):
    return (jax.nn.silu(t @ w_gate) * (t @ w_up)) @ w_down


def moe_swiglu(h, w_router, w_gate, w_up, w_down):
    B, S, D = h.shape
    t = h.reshape(B * S, D)
    logits = jnp.dot(t, w_router, preferred_element_type=jnp.float32)
    top_val, top_idx = lax.top_k(logits, TOP_K)
    gates = jax.nn.softmax(top_val, axis=-1)
    combine = jnp.einsum('tk,tke->te', gates, jax.nn.one_hot(top_idx, N_EXPERTS, dtype=jnp.float32))
    out = jnp.zeros_like(t)
    for e in range(N_EXPERTS):
        out = out + combine[:, e:e + 1].astype(t.dtype) * swiglu(t, w_gate[e], w_up[e], w_down[e])
    return out.reshape(B, S, D)


def setup_inputs(seed: int = 0) -> dict:
    key = jax.random.key(seed)
    ks = jax.random.split(key, 16)
    n_a = (DEPTH + 1) // 2
    n_b = DEPTH // 2
    attn_w = N_HEADS * HEAD_DIM

    def w(k, shape, fan_in):
        return jax.random.normal(k, shape, jnp.float32) * (fan_in ** -0.5)

    x = jax.random.normal(ks[0], (BATCH, SEQ, D_MODEL), jnp.float32)
    positions = (jnp.arange(SEQ, dtype=jnp.int32)[None, :]
                 + jax.random.randint(ks[1], (BATCH, 1), 0, MAX_POS_OFFSET, dtype=jnp.int32))
    mix_norm = 1.0 + 0.02 * jax.random.normal(ks[2], (DEPTH, D_MODEL), jnp.float32)
    ffn_norm = 1.0 + 0.02 * jax.random.normal(ks[3], (DEPTH, D_MODEL), jnp.float32)
    dil_w_in = w(ks[4], (n_a, D_MODEL, N_GROUPS_A * 3 * attn_w), D_MODEL)
    dil_w_out = w(ks[5], (n_a, attn_w, D_MODEL), attn_w)
    moba_w_in = w(ks[6], (n_b, D_MODEL, 3 * attn_w), D_MODEL)
    moba_w_out = w(ks[7], (n_b, attn_w, D_MODEL), attn_w)
    ffn_w_gate = w(ks[8], (n_a, D_MODEL, D_FF), D_MODEL)
    ffn_w_up = w(ks[9], (n_a, D_MODEL, D_FF), D_MODEL)
    ffn_w_down = w(ks[10], (n_a, D_FF, D_MODEL), D_FF)
    router_w = w(ks[11], (n_b, D_MODEL, N_EXPERTS), D_MODEL)
    exp_w_gate = w(ks[12], (n_b, N_EXPERTS, D_MODEL, D_FF_EXPERT), D_MODEL)
    exp_w_up = w(ks[13], (n_b, N_EXPERTS, D_MODEL, D_FF_EXPERT), D_MODEL)
    exp_w_down = w(ks[14], (n_b, N_EXPERTS, D_FF_EXPERT, D_MODEL), D_FF_EXPERT)
    final_norm = 1.0 + 0.02 * jax.random.normal(ks[15], (D_MODEL,), jnp.float32)
    return {"x": x, "positions": positions, "mix_norm": mix_norm, "ffn_norm": ffn_norm,
            "dil_w_in": dil_w_in, "dil_w_out": dil_w_out, "moba_w_in": moba_w_in,
            "moba_w_out": moba_w_out, "ffn_w_gate": ffn_w_gate, "ffn_w_up": ffn_w_up,
            "ffn_w_down": ffn_w_down, "router_w": router_w, "exp_w_gate": exp_w_gate,
            "exp_w_up": exp_w_up, "exp_w_down": exp_w_down, "final_norm": final_norm}


def reference(x, positions, mix_norm, ffn_norm, dil_w_in, dil_w_out, moba_w_in, moba_w_out,
              ffn_w_gate, ffn_w_up, ffn_w_down, router_w, exp_w_gate, exp_w_up, exp_w_down,
              final_norm):
    h = x
    for i in range(DEPTH):
        j = i // N_MIXERS
        hn = rms_norm(h, mix_norm[i])
        if i % N_MIXERS == 0:
            mix = dilated_mixer(hn, positions, dil_w_in[j], dil_w_out[j])
        else:
            mix = moba_mixer(hn, positions, moba_w_in[j], moba_w_out[j])
        h = h + mix.astype(h.dtype)
        hn = rms_norm(h, ffn_norm[i])
        jf = i // FFN_PERIOD
        if i % FFN_PERIOD == 0:
            ffn = swiglu(hn, ffn_w_gate[jf], ffn_w_up[jf], ffn_w_down[jf])
        else:
            ffn = moe_swiglu(hn, router_w[jf], exp_w_gate[jf], exp_w_up[jf], exp_w_down[jf])
        h = h + ffn.astype(h.dtype)
    return rms_norm(h, final_norm)
```

```python
import functools

import jax
import jax.numpy as jnp
from jax import lax
from jax.experimental import pallas as pl
from jax.experimental.pallas import tpu as pltpu

HEAD_DIM = 128
ROPE_DIM = HEAD_DIM // 4
ROPE_HALF = ROPE_DIM // 2
ROPE_THETA = 500000.0
SCALE = HEAD_DIM ** -0.5
NORM_EPS = 1e-5
TINY = 1e-30
DILATED_PAIRS = ((128, 1), (512, 4), (2048, 16))
MOBA_BLOCK = 256
MOBA_TOPK = 3
TOP_K = 2

LANES = 128
VMEM_LIMIT_BYTES = 56 * 2 ** 20

NEG_INF = float("-inf")
F32 = jnp.float32
BF16 = jnp.bfloat16

_NT_DIMS = (((1,), (1,)), ((), ()))


def _params(*sem):
    return pltpu.CompilerParams(dimension_semantics=sem, vmem_limit_bytes=VMEM_LIMIT_BYTES)


def _rope_table_kernel(pos_ref, invf_ref, c_ref, s1_ref, s2_ref):
    ang = pos_ref[...] * invf_ref[...]
    lane = lax.broadcasted_iota(jnp.int32, ang.shape, 1)
    c = jnp.cos(ang)
    s = jnp.sin(ang)
    c_ref[...] = jnp.where(lane < ROPE_DIM, c, 1.0)
    s1_ref[...] = jnp.where(lane < ROPE_HALF, -s, 0.0)
    s2_ref[...] = jnp.where(lane < ROPE_HALF, 0.0, jnp.where(lane < ROPE_DIM, s, 0.0))


def _rope_tables(positions):
    t = positions.size
    tm = 1024
    pos = positions.reshape(t, 1).astype(F32)
    inv_freq = jnp.power(ROPE_THETA, -jnp.arange(ROPE_HALF, dtype=F32) / ROPE_HALF)
    invf = jnp.zeros((1, LANES), F32).at[0, :ROPE_DIM].set(jnp.tile(inv_freq, 2))
    tab = jax.ShapeDtypeStruct((t, LANES), F32)
    spec = pl.BlockSpec((tm, LANES), lambda i: (i, 0))
    return pl.pallas_call(
        _rope_table_kernel,
        grid=(t // tm,),
        in_specs=[pl.BlockSpec((tm, 1), lambda i: (i, 0)), pl.BlockSpec((1, LANES), lambda i: (0, 0))],
        out_specs=[spec, spec, spec],
        out_shape=[tab, tab, tab],
        compiler_params=_params("parallel"),
        name="rope_tables",
    )(pos, invf)


def _rms(x, g):
    ms = jnp.mean(x * x, axis=-1, keepdims=True)
    return x * lax.rsqrt(ms + NORM_EPS) * g


def _rmsnorm_kernel(x_ref, g_ref, o_ref):
    o_ref[...] = _rms(x_ref[...], g_ref[...]).astype(o_ref.dtype)


def _rmsnorm(x, g, out_dtype=BF16):
    t, d = x.shape
    tm = 512
    return pl.pallas_call(
        _rmsnorm_kernel,
        grid=(t // tm,),
        in_specs=[pl.BlockSpec((tm, d), lambda i: (i, 0)), pl.BlockSpec((1, d), lambda i: (0, 0))],
        out_specs=pl.BlockSpec((tm, d), lambda i: (i, 0)),
        out_shape=jax.ShapeDtypeStruct((t, d), out_dtype),
        compiler_params=_params("parallel"),
        name="rmsnorm",
    )(x, g.reshape(1, d))


CAST_ROWS = 256


def _cast_weight(w_ref, wb_ref):
    def body(c, carry):
        r = pl.multiple_of(c * CAST_ROWS, CAST_ROWS)
        wb_ref[pl.ds(r, CAST_ROWS), :] = w_ref[0, pl.ds(r, CAST_ROWS), :].astype(wb_ref.dtype)
        return carry
    lax.fori_loop(0, wb_ref.shape[0] // CAST_ROWS, body, 0)


def _rope_rotate(acc, c, s1, s2):
    return (acc * c + pltpu.roll(acc, LANES - ROPE_HALF, 1) * s1
            + pltpu.roll(acc, ROPE_HALF, 1) * s2)


def _mm_kernel(grp_ref, first_ref, valid_ref, x_ref, *rest, mode, qkv_width):
    del grp_ref
    j = pl.program_id(0)
    i = pl.program_id(1)
    if mode == "swiglu":
        wg_ref, wu_ref, o_ref, wgb, wub = rest
        weights = ((wg_ref, wgb), (wu_ref, wub))
    elif mode == "residual":
        w_ref, res_ref, o_ref, wb = rest
        weights = ((w_ref, wb),)
    elif mode == "rope":
        w_ref, c_ref, s1_ref, s2_ref, o_ref, wb = rest
        weights = ((w_ref, wb),)
    else:
        w_ref, o_ref, wb = rest
        weights = ((w_ref, wb),)

    @pl.when(first_ref[i] == 1)
    def _cast():
        for w, b in weights:
            _cast_weight(w, b)

    @pl.when(valid_ref[i] == 0)
    def _skip():
        o_ref[...] = jnp.zeros(o_ref.shape, o_ref.dtype)

    @pl.when(valid_ref[i] == 1)
    def _compute():
        x = x_ref[...]
        if mode == "swiglu":
            g = jnp.dot(x, wgb[...], preferred_element_type=F32)
            u = jnp.dot(x, wub[...], preferred_element_type=F32)
            o_ref[...] = (g * (1.0 / (1.0 + jnp.exp(-g))) * u).astype(o_ref.dtype)
            return
        acc = jnp.dot(x, wb[...], preferred_element_type=F32)
        if mode == "residual":
            o_ref[...] = res_ref[...] + acc
        elif mode == "rope":
            tn = o_ref.shape[1]
            is_v = ((j * tn) // qkv_width) % 3 == 2

            @pl.when(is_v)
            def _():
                o_ref[...] = acc.astype(o_ref.dtype)

            @pl.when(jnp.logical_not(is_v))
            def _():
                c, s1, s2 = c_ref[...], s1_ref[...], s2_ref[...]
                for hh in range(tn // HEAD_DIM):
                    cs = slice(hh * HEAD_DIM, (hh + 1) * HEAD_DIM)
                    o_ref[:, cs] = _rope_rotate(acc[:, cs], c, s1, s2).astype(o_ref.dtype)
        else:
            o_ref[...] = acc.astype(o_ref.dtype)


def _dense_plan(n_tiles):
    grp = jnp.zeros((n_tiles,), jnp.int32)
    first = jnp.zeros((n_tiles,), jnp.int32).at[0].set(1)
    valid = jnp.ones((n_tiles,), jnp.int32)
    return grp, first, valid


def _matmul(x, ws, plan, *, mode, tm, tn, out_dtype, k_blk=None, k_idx=0, extras=(), qkv_width=0):
    m = x.shape[0]
    _, k, n = ws[0].shape
    kb = k if k_blk is None else k_blk
    grp, first, valid = plan
    x_spec = pl.BlockSpec((tm, kb), lambda j, i, g, f, v: (i, k_idx))
    w_spec = pl.BlockSpec((1, kb, tn), lambda j, i, g, f, v: (g[i], k_idx, j))
    o_spec = pl.BlockSpec((tm, tn), lambda j, i, g, f, v: (i, j))
    in_specs = [x_spec] + [w_spec] * len(ws)
    if mode == "residual":
        in_specs.append(o_spec)
    elif mode == "rope":
        in_specs += [pl.BlockSpec((tm, LANES), lambda j, i, g, f, v: (i, 0))] * 3
    grid_spec = pltpu.PrefetchScalarGridSpec(
        num_scalar_prefetch=3,
        grid=(n // tn, m // tm),
        in_specs=in_specs,
        out_specs=o_spec,
        scratch_shapes=[pltpu.VMEM((kb, tn), BF16)] * len(ws),
    )
    return pl.pallas_call(
        functools.partial(_mm_kernel, mode=mode, qkv_width=qkv_width),
        grid_spec=grid_spec,
        out_shape=jax.ShapeDtypeStruct((m, n), out_dtype),
        compiler_params=_params("arbitrary", "arbitrary"),
        name="mm_" + mode,
    )(grp, first, valid, x, *ws, *extras)


SPAN = 128
HEADS_PER_STEP = {1: 2, 4: 4, 16: 8}


def _dil_kernel(*refs, seq, heads, has_prev, emit_lse):
    q_ref, k_ref, v_ref = refs[:3]
    pos = 3
    if has_prev:
        op_ref, lp_ref = refs[3:5]
        pos = 5
    o_ref = refs[pos]
    l_ref = refs[pos + 1] if emit_lse else None

    row = lax.broadcasted_iota(jnp.int32, (SPAN, SPAN), 0)
    col = lax.broadcasted_iota(jnp.int32, (SPAN, SPAN), 1)
    mask_cur = col <= row
    mask_prev = col >= row

    def block(n, hh):
        r0 = pl.multiple_of(n * SPAN, SPAN)
        rp = pl.multiple_of(jnp.maximum(n - 1, 0) * SPAN, SPAN)
        cs = slice(hh * HEAD_DIM, (hh + 1) * HEAD_DIM)
        q = q_ref[0, pl.ds(r0, SPAN), cs]
        kc = k_ref[0, pl.ds(r0, SPAN), cs]
        kp = k_ref[0, pl.ds(rp, SPAN), cs]
        vc = v_ref[0, pl.ds(r0, SPAN), cs]
        vp = v_ref[0, pl.ds(rp, SPAN), cs]
        sc = lax.dot_general(q, kc, _NT_DIMS, preferred_element_type=F32) * SCALE
        sp = lax.dot_general(q, kp, _NT_DIMS, preferred_element_type=F32) * SCALE
        before_start = jnp.where(n > 0, 0.0, NEG_INF)
        sc = jnp.where(mask_cur, sc, NEG_INF)
        sp = jnp.where(mask_prev, sp + before_start, NEG_INF)
        m = jnp.maximum(jnp.max(sc, axis=1, keepdims=True), jnp.max(sp, axis=1, keepdims=True))
        pc = jnp.exp(sc - m)
        pp = jnp.exp(sp - m)
        den = jnp.sum(pc, axis=1, keepdims=True) + jnp.sum(pp, axis=1, keepdims=True)
        num = (jnp.dot(pc.astype(BF16), vc, preferred_element_type=F32)
               + jnp.dot(pp.astype(BF16), vp, preferred_element_type=F32))
        o = num / jnp.maximum(den, TINY)
        lse = m + jnp.log(den)
        if has_prev:
            lp = lp_ref[0, pl.ds(r0, SPAN), cs]
            op = op_ref[0, pl.ds(r0, SPAN), cs].astype(F32)
            mx = jnp.maximum(lp, lse)
            wp = jnp.exp(lp - mx)
            wn = jnp.exp(lse - mx)
            tot = wp + wn
            o = (op * wp + o * wn) / tot
            lse = mx + jnp.log(tot)
        o_ref[0, pl.ds(r0, SPAN), cs] = o.astype(o_ref.dtype)
        if emit_lse:
            l_ref[0, pl.ds(r0, SPAN), cs] = jnp.broadcast_to(lse, (SPAN, HEAD_DIM))

    for hh in range(heads):
        def body(n, carry, hh=hh):
            block(n, hh)
            return carry
        lax.fori_loop(0, seq // SPAN, body, 0)


def _dilated_group(proj, state, *, g, batch, attn_w):
    window, r = DILATED_PAIRS[g]
    assert window // r == SPAN
    t, e = proj.shape
    s = t // batch
    seq = s // r
    assert seq % SPAN == 0
    heads = HEADS_PER_STEP[r]
    w = heads * HEAD_DIM
    has_prev = state is not None
    emit_lse = g < len(DILATED_PAIRS) - 1
    pv = proj.reshape(batch, seq, r * e)
    ncb = e // w
    nsb = attn_w // w

    def qkv_spec(c):
        base = ((g * 3 + c) * attn_w) // w
        return pl.BlockSpec((1, seq, w), lambda b, rr, hc: (b, 0, rr * ncb + base + hc))

    st_spec = pl.BlockSpec((1, seq, w), lambda b, rr, hc: (b, 0, rr * nsb + hc))
    in_specs = [qkv_spec(0), qkv_spec(1), qkv_spec(2)]
    args = [pv, pv, pv]
    if has_prev:
        in_specs += [st_spec, st_spec]
        args += [state[0].reshape(batch, seq, r * attn_w), state[1].reshape(batch, seq, r * attn_w)]
    out_shape = [jax.ShapeDtypeStruct((batch, seq, r * attn_w), BF16)]
    out_specs = [st_spec]
    if emit_lse:
        out_shape.append(jax.ShapeDtypeStruct((batch, seq, r * attn_w), F32))
        out_specs.append(st_spec)
    outs = pl.pallas_call(
        functools.partial(_dil_kernel, seq=seq, heads=heads, has_prev=has_prev, emit_lse=emit_lse),
        grid=(batch, r, attn_w // w),
        in_specs=in_specs,
        out_specs=out_specs,
        out_shape=out_shape,
        compiler_params=_params("parallel", "parallel", "parallel"),
        name="dilated_attn_g%d" % g,
    )(*args)
    o = outs[0].reshape(t, attn_w)
    lse = outs[1].reshape(t, attn_w) if emit_lse else None
    return o, lse


def _dilated_attention(proj, batch, attn_w):
    state = None
    for g in range(len(DILATED_PAIRS)):
        state = _dilated_group(proj, state, g=g, batch=batch, attn_w=attn_w)
    return state[0]


def _moba_kernel(q_ref, k_ref, v_ref, o_ref, kmean_ref, *, seq):
    blk = MOBA_BLOCK
    nb = seq // blk
    kmean_ref[...] = jnp.zeros(kmean_ref.shape, F32)

    def km_body(jb, carry):
        r0 = pl.multiple_of(jb * blk, blk)
        kb = k_ref[0, pl.ds(r0, blk), :].astype(F32)
        kmean_ref[pl.ds(jb, 1), :] = jnp.sum(kb, axis=0, keepdims=True) * (1.0 / blk)
        return carry
    lax.fori_loop(0, nb, km_body, 0)

    lane = lax.broadcasted_iota(jnp.int32, (blk, LANES), 1)
    row = lax.broadcasted_iota(jnp.int32, (blk, blk), 0)
    col = lax.broadcasted_iota(jnp.int32, (blk, blk), 1)
    causal = col <= row

    def q_body(qi, carry):
        r0 = pl.multiple_of(qi * blk, blk)
        q = q_ref[0, pl.ds(r0, blk), :]
        gate = lax.dot_general(q.astype(F32), kmean_ref[...], _NT_DIMS,
                               preferred_element_type=F32, precision=lax.Precision.HIGHEST)
        gate = jnp.where(lane < qi, gate, NEG_INF)
        beaten = jnp.zeros((blk, LANES), F32)
        for jp in range(nb):
            cj = gate[:, jp:jp + 1]
            tie = jnp.where(lane > jp, 1.0, 0.0)
            beaten = beaten + jnp.where(cj > gate, 1.0, jnp.where(cj == gate, tie, 0.0))
        sel = jnp.where(beaten < MOBA_TOPK, jnp.where(lane < qi, 1.0, 0.0), 0.0)

        s = lax.dot_general(q, k_ref[0, pl.ds(r0, blk), :], _NT_DIMS, preferred_element_type=F32) * SCALE
        s = jnp.where(causal, s, NEG_INF)
        m = jnp.max(s, axis=1, keepdims=True)
        p = jnp.exp(s - m)
        l = jnp.sum(p, axis=1, keepdims=True)
        acc = jnp.dot(p.astype(BF16), v_ref[0, pl.ds(r0, blk), :], preferred_element_type=F32)

        def past(jb, st):
            m, l, acc = st
            c0 = pl.multiple_of(jb * blk, blk)
            chosen = jnp.max(jnp.where(lane == jb, sel, 0.0), axis=1, keepdims=True)
            bias = jnp.where(chosen > 0.0, 0.0, NEG_INF)
            s = lax.dot_general(q, k_ref[0, pl.ds(c0, blk), :], _NT_DIMS,
                                preferred_element_type=F32) * SCALE + bias
            m_new = jnp.maximum(m, jnp.max(s, axis=1, keepdims=True))
            alpha = jnp.exp(m - m_new)
            p = jnp.exp(s - m_new)
            l = alpha * l + jnp.sum(p, axis=1, keepdims=True)
            acc = alpha * acc + jnp.dot(p.astype(BF16), v_ref[0, pl.ds(c0, blk), :],
                                        preferred_element_type=F32)
            return m_new, l, acc

        m, l, acc = lax.fori_loop(0, qi, past, (m, l, acc))
        o_ref[0, pl.ds(r0, blk), :] = (acc / jnp.maximum(l, TINY)).astype(o_ref.dtype)
        return carry

    lax.fori_loop(0, nb, q_body, 0)


def _moba_attention(proj, batch, attn_w):
    t, e = proj.shape
    s = t // batch
    assert s % MOBA_BLOCK == 0 and s // MOBA_BLOCK <= LANES
    n_heads = attn_w // HEAD_DIM
    pv = proj.reshape(batch, s, e)

    def spec(c):
        return pl.BlockSpec((1, s, HEAD_DIM), lambda b, h: (b, 0, c * n_heads + h))

    out = pl.pallas_call(
        functools.partial(_moba_kernel, seq=s),
        grid=(batch, n_heads),
        in_specs=[spec(0), spec(1), spec(2)],
        out_specs=pl.BlockSpec((1, s, HEAD_DIM), lambda b, h: (b, 0, h)),
        out_shape=jax.ShapeDtypeStruct((batch, s, attn_w), BF16),
        scratch_shapes=[pltpu.VMEM((LANES, HEAD_DIM), F32)],
        compiler_params=_params("parallel", "parallel"),
        name="moba_attn",
    )(pv, pv, pv)
    return out.reshape(t, attn_w)


def _router_kernel(x_ref, g_ref, wr_ref, i1_ref, i2_ref, g1_ref, g2_ref, *, n_experts):
    y = _rms(x_ref[...], g_ref[...])
    lg = jnp.dot(y, wr_ref[...], preferred_element_type=F32, precision=lax.Precision.HIGHEST)
    lane = lax.broadcasted_iota(jnp.int32, lg.shape, 1).astype(F32)
    lg = jnp.where(lane < n_experts, lg, NEG_INF)
    m1 = jnp.max(lg, axis=1, keepdims=True)
    i1 = jnp.min(jnp.where(lg == m1, lane, float(LANES)), axis=1, keepdims=True)
    lg2 = jnp.where(lane == i1, NEG_INF, lg)
    m2 = jnp.max(lg2, axis=1, keepdims=True)
    i2 = jnp.min(jnp.where(lg2 == m2, lane, float(LANES)), axis=1, keepdims=True)
    e = jnp.exp(m2 - m1)
    i1_ref[...] = i1.astype(jnp.int32)
    i2_ref[...] = i2.astype(jnp.int32)
    g1_ref[...] = 1.0 / (1.0 + e)
    g2_ref[...] = e / (1.0 + e)


def _router(h, g, w_router):
    t, d = h.shape
    n_experts = w_router.shape[1]
    assert n_experts <= LANES
    tm = 256
    wr = jnp.zeros((d, LANES), F32).at[:, :n_experts].set(w_router)
    col = pl.BlockSpec((tm, 1), lambda i: (i, 0))
    return pl.pallas_call(
        functools.partial(_router_kernel, n_experts=n_experts),
        grid=(t // tm,),
        in_specs=[pl.BlockSpec((tm, d), lambda i: (i, 0)), pl.BlockSpec((1, d), lambda i: (0, 0)),
                  pl.BlockSpec((d, LANES), lambda i: (0, 0))],
        out_specs=[col, col, col, col],
        out_shape=[jax.ShapeDtypeStruct((t, 1), jnp.int32)] * 2 + [jax.ShapeDtypeStruct((t, 1), F32)] * 2,
        compiler_params=_params("parallel"),
        name="router",
    )(h, g.reshape(1, d), wr)


def _route_plan(i1, i2, n_experts, tm):
    t = i1.shape[0]
    e = jnp.concatenate([i1, i2])
    onehot = (e[:, None] == jnp.arange(n_experts, dtype=jnp.int32)[None, :]).astype(jnp.int32)
    csum = jnp.cumsum(onehot, axis=0)
    rank = jnp.take_along_axis(csum, e[:, None], axis=1)[:, 0] - 1
    counts = csum[-1]
    padded = ((counts + tm - 1) // tm) * tm
    ends = jnp.cumsum(padded)
    starts = ends - padded
    pos = starts[e] + rank
    n_rows = TOP_K * t + n_experts * tm
    tok = jnp.arange(t, dtype=jnp.int32)
    row_token = jnp.zeros((n_rows,), jnp.int32).at[pos].set(jnp.concatenate([tok, tok]))
    tile_start = jnp.arange(n_rows // tm, dtype=jnp.int32) * tm
    grp = jnp.sum((tile_start[:, None] >= ends[None, :]).astype(jnp.int32), axis=1)
    in_use = grp < n_experts
    grp = jnp.minimum(grp, n_experts - 1)
    valid = (in_use & (tile_start < (starts + counts)[grp])).astype(jnp.int32)
    first = jnp.concatenate([jnp.ones((1,), jnp.int32), (grp[1:] != grp[:-1]).astype(jnp.int32)])
    return (grp, first, valid), row_token, pos[:t], pos[t:]


def _gather_norm_kernel(tok_ref, valid_ref, g_ref, h_hbm, o_ref, buf, sem, *, tm):
    i = pl.program_id(0)

    def row_copy(r):
        return pltpu.make_async_copy(h_hbm.at[pl.ds(tok_ref[i * tm + r], 1)], buf.at[pl.ds(r, 1)], sem)

    @pl.when(valid_ref[i] == 0)
    def _skip():
        o_ref[...] = jnp.zeros(o_ref.shape, o_ref.dtype)

    @pl.when(valid_ref[i] == 1)
    def _gather():
        def start(r, carry):
            row_copy(r).start()
            return carry
        lax.fori_loop(0, tm, start, 0)

        def wait(r, carry):
            row_copy(r).wait()
            return carry
        lax.fori_loop(0, tm, wait, 0)
        o_ref[...] = _rms(buf[...], g_ref[...]).astype(o_ref.dtype)


def _gather_norm(h, g, row_token, valid, tm):
    t, d = h.shape
    n_rows = row_token.shape[0]
    grid_spec = pltpu.PrefetchScalarGridSpec(
        num_scalar_prefetch=2,
        grid=(n_rows // tm,),
        in_specs=[pl.BlockSpec((1, d), lambda i, tok, v: (0, 0)), pl.BlockSpec(memory_space=pl.ANY)],
        out_specs=pl.BlockSpec((tm, d), lambda i, tok, v: (i, 0)),
        scratch_shapes=[pltpu.VMEM((tm, d), F32), pltpu.SemaphoreType.DMA(())],
    )
    return pl.pallas_call(
        functools.partial(_gather_norm_kernel, tm=tm),
        grid_spec=grid_spec,
        out_shape=jax.ShapeDtypeStruct((n_rows, d), BF16),
        compiler_params=_params("arbitrary"),
        name="moe_gather_norm",
    )(row_token, valid, g.reshape(1, d), h)


def _combine_kernel(p1_ref, p2_ref, h_ref, g1_ref, g2_ref, fn_ref, y_hbm, o_ref, buf, sem, *, tm):
    i = pl.program_id(0)

    def row_copy(r, which, p_ref):
        return pltpu.make_async_copy(y_hbm.at[pl.ds(p_ref[i * tm + r], 1)], buf.at[which, pl.ds(r, 1)], sem)

    def start(r, carry):
        row_copy(r, 0, p1_ref).start()
        row_copy(r, 1, p2_ref).start()
        return carry
    lax.fori_loop(0, tm, start, 0)

    def wait(r, carry):
        row_copy(r, 0, p1_ref).wait()
        row_copy(r, 1, p2_ref).wait()
        return carry
    lax.fori_loop(0, tm, wait, 0)
    h = h_ref[...] + (g1_ref[...] * buf[0] + g2_ref[...] * buf[1])
    o_ref[...] = _rms(h, fn_ref[...])


def _combine_norm(h, y, pos1, pos2, g1, g2, final_norm):
    t, d = h.shape
    tm = 256
    col = pl.BlockSpec((tm, 1), lambda i, a, b: (i, 0))
    row = pl.BlockSpec((tm, d), lambda i, a, b: (i, 0))
    grid_spec = pltpu.PrefetchScalarGridSpec(
        num_scalar_prefetch=2,
        grid=(t // tm,),
        in_specs=[row, col, col, pl.BlockSpec((1, d), lambda i, a, b: (0, 0)),
                  pl.BlockSpec(memory_space=pl.ANY)],
        out_specs=row,
        scratch_shapes=[pltpu.VMEM((TOP_K, tm, d), F32), pltpu.SemaphoreType.DMA(())],
    )
    return pl.pallas_call(
        functools.partial(_combine_kernel, tm=tm),
        grid_spec=grid_spec,
        out_shape=jax.ShapeDtypeStruct((t, d), F32),
        compiler_params=_params("arbitrary"),
        name="moe_combine_norm",
    )(pos1, pos2, h, g1, g2, final_norm.reshape(1, d), y)


TM_DENSE = 1024
TN_DENSE = 512
TN_SWIGLU = 256
TM_EXPERT = 512
DOWN_K_SPLITS = 4


def _down_proj(a, w, plan, res, tm, tn):
    k = w.shape[1]
    splits = DOWN_K_SPLITS if k > 8192 else 1
    kb = k // splits
    for ki in range(splits):
        res = _matmul(a, (w,), plan, mode="residual", tm=tm, tn=tn, out_dtype=F32,
                      k_blk=kb, k_idx=ki, extras=(res,))
    return res


def kernel(x, positions, mix_norm, ffn_norm, dil_w_in, dil_w_out, moba_w_in, moba_w_out,
           ffn_w_gate, ffn_w_up, ffn_w_down, router_w, exp_w_gate, exp_w_up, exp_w_down,
           final_norm):
    batch, s, d = x.shape
    t = batch * s
    attn_w = dil_w_out.shape[1]
    h = x.reshape(t, d)
    rope = _rope_tables(positions)
    dense = _dense_plan(t // TM_DENSE)

    hn = _rmsnorm(h, mix_norm[0])
    proj = _matmul(hn, (dil_w_in,), dense, mode="rope", tm=TM_DENSE, tn=TN_DENSE, out_dtype=BF16,
                   extras=rope, qkv_width=attn_w)
    o = _dilated_attention(proj, batch, attn_w)
    h = _matmul(o, (dil_w_out,), dense, mode="residual", tm=TM_DENSE, tn=TN_DENSE, out_dtype=F32,
                extras=(h,))
    hn = _rmsnorm(h, ffn_norm[0])
    a = _matmul(hn, (ffn_w_gate, ffn_w_up), dense, mode="swiglu", tm=TM_DENSE, tn=TN_SWIGLU,
                out_dtype=BF16)
    h = _down_proj(a, ffn_w_down, dense, h, TM_DENSE, TN_DENSE)

    hn = _rmsnorm(h, mix_norm[1])
    proj = _matmul(hn, (moba_w_in,), dense, mode="rope", tm=TM_DENSE, tn=TN_DENSE, out_dtype=BF16,
                   extras=rope, qkv_width=attn_w)
    o = _moba_attention(proj, batch, attn_w)
    h = _matmul(o, (moba_w_out,), dense, mode="residual", tm=TM_DENSE, tn=TN_DENSE, out_dtype=F32,
                extras=(h,))

    n_experts = router_w.shape[-1]
    i1, i2, g1, g2 = _router(h, ffn_norm[1], router_w[0])
    plan, row_token, pos1, pos2 = _route_plan(i1[:, 0], i2[:, 0], n_experts, TM_EXPERT)
    xs = _gather_norm(h, ffn_norm[1], row_token, plan[2], TM_EXPERT)
    a = _matmul(xs, (exp_w_gate[0], exp_w_up[0]), plan, mode="swiglu", tm=TM_EXPERT, tn=TN_SWIGLU,
                out_dtype=BF16)
    y = _matmul(a, (exp_w_down[0],), plan, mode="plain", tm=TM_EXPERT, tn=TN_DENSE, out_dtype=F32)
    out = _combine_norm(h, y, pos1, pos2, g1, g2, final_norm)
    return out.reshape(batch, s, d)
```

```python
import functools

import jax
import jax.numpy as jnp
from jax import lax
from jax.experimental import pallas as pl
from jax.experimental.pallas import tpu as pltpu

HEAD_DIM = 128
ROPE_DIM = HEAD_DIM // 4
ROPE_HALF = ROPE_DIM // 2
ROPE_THETA = 500000.0
SCALE = HEAD_DIM ** -0.5
NORM_EPS = 1e-5
TINY = 1e-30
DILATED_PAIRS = ((128, 1), (512, 4), (2048, 16))
MOBA_BLOCK = 256
MOBA_TOPK = 3
TOP_K = 2

LANES = 128
SUBLANES = 8
VMEM_LIMIT_BYTES = 56 * 2 ** 20

NEG_INF = float("-inf")
F32 = jnp.float32
BF16 = jnp.bfloat16

_NT_DIMS = (((1,), (1,)), ((), ()))


def _params(*sem):
    return pltpu.CompilerParams(dimension_semantics=sem, vmem_limit_bytes=VMEM_LIMIT_BYTES)


def _rope_table_kernel(pos_ref, invf_ref, c_ref, s1_ref, s2_ref):
    ang = pos_ref[...] * invf_ref[...]
    lane = lax.broadcasted_iota(jnp.int32, ang.shape, 1)
    c = jnp.cos(ang)
    s = jnp.sin(ang)
    c_ref[...] = jnp.where(lane < ROPE_DIM, c, 1.0)
    s1_ref[...] = jnp.where(lane < ROPE_HALF, -s, 0.0)
    s2_ref[...] = jnp.where(lane < ROPE_HALF, 0.0, jnp.where(lane < ROPE_DIM, s, 0.0))


def _rope_tables(positions):
    t = positions.size
    tm = 1024
    pos = positions.reshape(t, 1).astype(F32)
    inv_freq = jnp.power(ROPE_THETA, -jnp.arange(ROPE_HALF, dtype=F32) / ROPE_HALF)
    invf = jnp.zeros((1, LANES), F32).at[0, :ROPE_DIM].set(jnp.tile(inv_freq, 2))
    tab = jax.ShapeDtypeStruct((t, LANES), F32)
    spec = pl.BlockSpec((tm, LANES), lambda i: (i, 0))
    return pl.pallas_call(
        _rope_table_kernel,
        grid=(t // tm,),
        in_specs=[pl.BlockSpec((tm, 1), lambda i: (i, 0)), pl.BlockSpec((1, LANES), lambda i: (0, 0))],
        out_specs=[spec, spec, spec],
        out_shape=[tab, tab, tab],
        compiler_params=_params("parallel"),
        name="rope_tables",
    )(pos, invf)


def _rms(x, g):
    ms = jnp.mean(x * x, axis=-1, keepdims=True)
    return x * lax.rsqrt(ms + NORM_EPS) * g


def _rmsnorm_kernel(x_ref, g_ref, o_ref):
    o_ref[...] = _rms(x_ref[...], g_ref[...]).astype(o_ref.dtype)


def _rmsnorm(x, g, out_dtype=BF16):
    t, d = x.shape
    tm = 512
    return pl.pallas_call(
        _rmsnorm_kernel,
        grid=(t // tm,),
        in_specs=[pl.BlockSpec((tm, d), lambda i: (i, 0)), pl.BlockSpec((1, d), lambda i: (0, 0))],
        out_specs=pl.BlockSpec((tm, d), lambda i: (i, 0)),
        out_shape=jax.ShapeDtypeStruct((t, d), out_dtype),
        compiler_params=_params("parallel"),
        name="rmsnorm",
    )(x, g.reshape(1, d))


CAST_ROWS = 256
MM_ROW_CHUNK = 256


def _cast_weight(w_ref, wb_ref):
    def body(c, carry):
        r = pl.multiple_of(c * CAST_ROWS, CAST_ROWS)
        wb_ref[pl.ds(r, CAST_ROWS), :] = w_ref[0, pl.ds(r, CAST_ROWS), :].astype(wb_ref.dtype)
        return carry
    lax.fori_loop(0, wb_ref.shape[0] // CAST_ROWS, body, 0)


def _rope_rotate(acc, c, s1, s2):
    return (acc * c + pltpu.roll(acc, LANES - ROPE_HALF, 1) * s1
            + pltpu.roll(acc, ROPE_HALF, 1) * s2)


def _mm_kernel(grp_ref, first_ref, valid_ref, x_ref, *rest, mode, qkv_width):
    del grp_ref
    j = pl.program_id(0)
    i = pl.program_id(1)
    if mode == "swiglu":
        wg_ref, wu_ref, o_ref, wgb, wub = rest
        weights = ((wg_ref, wgb), (wu_ref, wub))
    elif mode == "residual":
        w_ref, res_ref, o_ref, wb = rest
        weights = ((w_ref, wb),)
    elif mode == "rope":
        w_ref, c_ref, s1_ref, s2_ref, o_ref, wb = rest
        weights = ((w_ref, wb),)
    else:
        w_ref, o_ref, wb = rest
        weights = ((w_ref, wb),)

    @pl.when(first_ref[i] == 1)
    def _cast():
        for w, b in weights:
            _cast_weight(w, b)

    @pl.when(valid_ref[i] == 0)
    def _skip():
        o_ref[...] = jnp.zeros(o_ref.shape, o_ref.dtype)

    @pl.when(valid_ref[i] == 1)
    def _compute():
        tm, tn = o_ref.shape
        for r0 in range(0, tm, MM_ROW_CHUNK):
            rows = slice(r0, r0 + MM_ROW_CHUNK)
            x = x_ref[rows, :]
            if mode == "swiglu":
                g = jnp.dot(x, wgb[...], preferred_element_type=F32)
                u = jnp.dot(x, wub[...], preferred_element_type=F32)
                o_ref[rows, :] = (g * (1.0 / (1.0 + jnp.exp(-g))) * u).astype(o_ref.dtype)
                continue
            acc = jnp.dot(x, wb[...], preferred_element_type=F32)
            if mode == "residual":
                o_ref[rows, :] = res_ref[rows, :] + acc
            elif mode == "rope":
                is_v = (((j * tn) // qkv_width) % 3 == 2).astype(jnp.int32)
                keep = jnp.broadcast_to(is_v, (MM_ROW_CHUNK, HEAD_DIM)) > 0
                c, s1, s2 = c_ref[rows, :], s1_ref[rows, :], s2_ref[rows, :]
                for hh in range(tn // HEAD_DIM):
                    cs = slice(hh * HEAD_DIM, (hh + 1) * HEAD_DIM)
                    a = acc[:, cs]
                    o_ref[rows, cs] = jnp.where(keep, a, _rope_rotate(a, c, s1, s2)).astype(o_ref.dtype)
            else:
                o_ref[rows, :] = acc.astype(o_ref.dtype)


def _dense_plan(n_tiles):
    grp = jnp.zeros((n_tiles,), jnp.int32)
    first = jnp.zeros((n_tiles,), jnp.int32).at[0].set(1)
    valid = jnp.ones((n_tiles,), jnp.int32)
    return grp, first, valid


def _matmul(x, ws, plan, *, mode, tm, tn, out_dtype, k_blk=None, k_idx=0, extras=(), qkv_width=0):
    m = x.shape[0]
    _, k, n = ws[0].shape
    kb = k if k_blk is None else k_blk
    grp, first, valid = plan
    x_spec = pl.BlockSpec((tm, kb), lambda j, i, g, f, v: (i, k_idx))
    w_spec = pl.BlockSpec((1, kb, tn), lambda j, i, g, f, v: (g[i], k_idx, j))
    o_spec = pl.BlockSpec((tm, tn), lambda j, i, g, f, v: (i, j))
    in_specs = [x_spec] + [w_spec] * len(ws)
    if mode == "residual":
        in_specs.append(o_spec)
    elif mode == "rope":
        in_specs += [pl.BlockSpec((tm, LANES), lambda j, i, g, f, v: (i, 0))] * 3
    grid_spec = pltpu.PrefetchScalarGridSpec(
        num_scalar_prefetch=3,
        grid=(n // tn, m // tm),
        in_specs=in_specs,
        out_specs=o_spec,
        scratch_shapes=[pltpu.VMEM((kb, tn), BF16)] * len(ws),
    )
    return pl.pallas_call(
        functools.partial(_mm_kernel, mode=mode, qkv_width=qkv_width),
        grid_spec=grid_spec,
        out_shape=jax.ShapeDtypeStruct((m, n), out_dtype),
        compiler_params=_params("arbitrary", "arbitrary"),
        name="mm_" + mode,
    )(grp, first, valid, x, *ws, *extras)


SPAN = 128


def _dil_kernel(*refs, seq):
    qkv = refs[:9]
    o_ref = refs[9]
    stage, qp, kp, vt, og, lg, o_run, l_run = refs[10:]
    n_blocks = seq // SPAN

    key = lax.broadcasted_iota(jnp.int32, (SPAN, SPAN), 0)
    qry = lax.broadcasted_iota(jnp.int32, (SPAN, SPAN), 1)
    bias_cur = jnp.where(key <= qry, 0.0, NEG_INF)
    bias_prev = jnp.where(key >= qry, 0.0, NEG_INF)

    for g, (window, r) in enumerate(DILATED_PAIRS):
        q_ref, k_ref, v_ref = qkv[3 * g:3 * g + 3]
        sub_len = seq // r
        nblk = sub_len // SPAN
        last = g == len(DILATED_PAIRS) - 1

        def regroup(src_ref, dst_ref, r=r, sub_len=sub_len):
            if r == 1:
                dst_ref[...] = src_ref[0].astype(dst_ref.dtype)
                return
            stage[...] = src_ref[0].astype(F32)
            for rr in range(r):
                dst_ref[pl.ds(rr * sub_len, sub_len), :] = (
                    stage[pl.ds(rr, sub_len, stride=r), :].astype(dst_ref.dtype))

        regroup(q_ref, qp)
        regroup(k_ref, kp)
        regroup(v_ref, og)

        def vt_body(i, carry):
            base = pl.multiple_of(i * SPAN, SPAN)
            vt[i] = og[pl.ds(base, SPAN), :].T.astype(BF16)
            return carry
        lax.fori_loop(0, n_blocks, vt_body, 0)

        out_o, out_l = (o_run, l_run) if g == 0 else (og, lg)

        def blk_body(i, carry, nblk=nblk, out_o=out_o, out_l=out_l):
            base = pl.multiple_of(i * SPAN, SPAN)
            ip = jnp.maximum(i - 1, 0)
            pbase = pl.multiple_of(ip * SPAN, SPAN)
            q = qp[pl.ds(base, SPAN), :]
            no_prev = jnp.where(lax.rem(i, nblk) > 0, 0.0, NEG_INF)
            s_c = lax.dot_general(kp[pl.ds(base, SPAN), :], q, _NT_DIMS,
                                  preferred_element_type=F32) * SCALE + bias_cur
            s_p = lax.dot_general(kp[pl.ds(pbase, SPAN), :], q, _NT_DIMS,
                                  preferred_element_type=F32) * SCALE + (bias_prev + no_prev)
            m = jnp.maximum(jnp.max(s_c, axis=0, keepdims=True), jnp.max(s_p, axis=0, keepdims=True))
            p_c = jnp.exp(s_c - m)
            p_p = jnp.exp(s_p - m)
            den = jnp.sum(p_c, axis=0, keepdims=True) + jnp.sum(p_p, axis=0, keepdims=True)
            o_t = (jnp.dot(vt[i], p_c.astype(BF16), preferred_element_type=F32)
                   + jnp.dot(vt[ip], p_p.astype(BF16), preferred_element_type=F32))
            o_t = o_t / jnp.maximum(den, TINY)
            lse = m + jnp.log(den)
            out_o[pl.ds(base, SPAN), :] = o_t.T
            out_l[pl.ds(base, SPAN), :] = jnp.broadcast_to(lse, (SPAN, SPAN)).T
            return carry
        lax.fori_loop(0, n_blocks, blk_body, 0, unroll=2)

        if g > 0:
            for rr in range(r):
                tok = pl.ds(rr, sub_len, stride=r)
                rows = pl.ds(rr * sub_len, sub_len)
                lp = l_run[tok, :]
                ln = lg[rows, :]
                mx = jnp.maximum(lp, ln)
                wp = jnp.exp(lp - mx)
                wn = jnp.exp(ln - mx)
                tot = wp + wn
                o_run[tok, :] = (o_run[tok, :] * wp + og[rows, :] * wn) / tot
                if not last:
                    l_run[tok, :] = mx + jnp.log(tot)

    o_ref[0] = o_run[...].astype(o_ref.dtype)


def _dilated_attention(proj, batch, attn_w):
    t, e = proj.shape
    s = t // batch
    n_heads = attn_w // HEAD_DIM
    for window, r in DILATED_PAIRS:
        assert window // r == SPAN and s % (r * SPAN) == 0
    pv = proj.reshape(batch, s, e)

    def spec(g, c):
        return pl.BlockSpec((1, s, HEAD_DIM), lambda b, h: (b, 0, (g * 3 + c) * n_heads + h))

    seq_f32 = pltpu.VMEM((s, HEAD_DIM), F32)
    seq_bf16 = pltpu.VMEM((s, HEAD_DIM), BF16)
    out = pl.pallas_call(
        functools.partial(_dil_kernel, seq=s),
        grid=(batch, n_heads),
        in_specs=[spec(g, c) for g in range(len(DILATED_PAIRS)) for c in range(3)],
        out_specs=pl.BlockSpec((1, s, HEAD_DIM), lambda b, h: (b, 0, h)),
        out_shape=jax.ShapeDtypeStruct((batch, s, attn_w), BF16),
        scratch_shapes=[seq_f32, seq_bf16, seq_bf16, pltpu.VMEM((s // SPAN, HEAD_DIM, SPAN), BF16),
                        seq_f32, seq_f32, seq_f32, seq_f32],
        compiler_params=_params("parallel", "parallel"),
        name="dilated_attn",
    )(*([pv] * 9))
    return out.reshape(t, attn_w)


def _moba_kernel(q_ref, k_ref, v_ref, o_ref, vt, kmean, sel_ref, *, seq):
    blk = MOBA_BLOCK
    nb = seq // blk
    nbp = kmean.shape[0]
    kmean[...] = jnp.zeros(kmean.shape, F32)

    def pre(jb, carry):
        r0 = pl.multiple_of(jb * blk, blk)
        vt[jb] = v_ref[0, pl.ds(r0, blk), :].astype(F32).T.astype(BF16)
        kb = k_ref[0, pl.ds(r0, blk), :].astype(F32)
        kmean[pl.ds(jb, 1), :] = jnp.sum(kb, axis=0, keepdims=True) * (1.0 / blk)
        return carry
    lax.fori_loop(0, nb, pre, 0)

    kblk = lax.broadcasted_iota(jnp.int32, (nbp, blk), 0)
    key = lax.broadcasted_iota(jnp.int32, (blk, blk), 0)
    qry = lax.broadcasted_iota(jnp.int32, (blk, blk), 1)
    bias_causal = jnp.where(key <= qry, 0.0, NEG_INF)

    def q_body(qi, carry):
        r0 = pl.multiple_of(qi * blk, blk)
        q = q_ref[0, pl.ds(r0, blk), :]
        gate = lax.dot_general(kmean[...], q.astype(F32), _NT_DIMS,
                               preferred_element_type=F32, precision=lax.Precision.HIGHEST)
        gate = jnp.where(kblk < qi, gate, NEG_INF)
        beaten = jnp.zeros((nbp, blk), F32)
        for jp in range(nb):
            gj = gate[jp:jp + 1, :]
            tie = jnp.where(kblk > jp, 1.0, 0.0)
            beaten = beaten + jnp.where(gj > gate, 1.0, jnp.where(gj == gate, tie, 0.0))
        sel_ref[...] = jnp.where(beaten < MOBA_TOPK, jnp.where(kblk < qi, 0.0, NEG_INF), NEG_INF)

        s = lax.dot_general(k_ref[0, pl.ds(r0, blk), :], q, _NT_DIMS,
                            preferred_element_type=F32) * SCALE + bias_causal
        m = jnp.max(s, axis=0, keepdims=True)
        p = jnp.exp(s - m)
        l = jnp.sum(p, axis=0, keepdims=True)
        acc = jnp.dot(vt[qi], p.astype(BF16), preferred_element_type=F32)

        def past(jb, st):
            m, l, acc = st
            c0 = pl.multiple_of(jb * blk, blk)
            s = lax.dot_general(k_ref[0, pl.ds(c0, blk), :], q, _NT_DIMS,
                                preferred_element_type=F32) * SCALE + sel_ref[pl.ds(jb, 1), :]
            m_new = jnp.maximum(m, jnp.max(s, axis=0, keepdims=True))
            alpha = jnp.exp(m - m_new)
            p = jnp.exp(s - m_new)
            l = alpha * l + jnp.sum(p, axis=0, keepdims=True)
            acc = alpha * acc + jnp.dot(vt[jb], p.astype(BF16), preferred_element_type=F32)
            return m_new, l, acc

        m, l, acc = lax.fori_loop(0, qi, past, (m, l, acc))
        o_ref[0, pl.ds(r0, blk), :] = (acc / jnp.maximum(l, TINY)).T.astype(o_ref.dtype)
        return carry

    lax.fori_loop(0, nb, q_body, 0)


def _moba_attention(proj, batch, attn_w):
    t, e = proj.shape
    s = t // batch
    assert s % MOBA_BLOCK == 0
    nb = s // MOBA_BLOCK
    nbp = -(-nb // SUBLANES) * SUBLANES
    n_heads = attn_w // HEAD_DIM
    pv = proj.reshape(batch, s, e)

    def spec(c):
        return pl.BlockSpec((1, s, HEAD_DIM), lambda b, h: (b, 0, c * n_heads + h))

    out = pl.pallas_call(
        functools.partial(_moba_kernel, seq=s),
        grid=(batch, n_heads),
        in_specs=[spec(0), spec(1), spec(2)],
        out_specs=pl.BlockSpec((1, s, HEAD_DIM), lambda b, h: (b, 0, h)),
        out_shape=jax.ShapeDtypeStruct((batch, s, attn_w), BF16),
        scratch_shapes=[pltpu.VMEM((nb, HEAD_DIM, MOBA_BLOCK), BF16),
                        pltpu.VMEM((nbp, HEAD_DIM), F32),
                        pltpu.VMEM((nbp, MOBA_BLOCK), F32)],
        compiler_params=_params("parallel", "parallel"),
        name="moba_attn",
    )(pv, pv, pv)
    return out.reshape(t, attn_w)


def _router_kernel(x_ref, g_ref, wr_ref, i1_ref, i2_ref, g1_ref, g2_ref, *, n_experts):
    y = _rms(x_ref[...], g_ref[...])
    lg = jnp.dot(y, wr_ref[...], preferred_element_type=F32, precision=lax.Precision.HIGHEST)
    lane = lax.broadcasted_iota(jnp.int32, lg.shape, 1).astype(F32)
    lg = jnp.where(lane < n_experts, lg, NEG_INF)
    m1 = jnp.max(lg, axis=1, keepdims=True)
    i1 = jnp.min(jnp.where(lg == m1, lane, float(LANES)), axis=1, keepdims=True)
    lg2 = jnp.where(lane == i1, NEG_INF, lg)
    m2 = jnp.max(lg2, axis=1, keepdims=True)
    i2 = jnp.min(jnp.where(lg2 == m2, lane, float(LANES)), axis=1, keepdims=True)
    e = jnp.exp(m2 - m1)
    i1_ref[...] = i1.astype(jnp.int32)
    i2_ref[...] = i2.astype(jnp.int32)
    g1_ref[...] = 1.0 / (1.0 + e)
    g2_ref[...] = e / (1.0 + e)


def _router(h, g, w_router):
    t, d = h.shape
    n_experts = w_router.shape[1]
    assert n_experts <= LANES
    tm = 256
    wr = jnp.zeros((d, LANES), F32).at[:, :n_experts].set(w_router)
    col = pl.BlockSpec((tm, 1), lambda i: (i, 0))
    return pl.pallas_call(
        functools.partial(_router_kernel, n_experts=n_experts),
        grid=(t // tm,),
        in_specs=[pl.BlockSpec((tm, d), lambda i: (i, 0)), pl.BlockSpec((1, d), lambda i: (0, 0)),
                  pl.BlockSpec((d, LANES), lambda i: (0, 0))],
        out_specs=[col, col, col, col],
        out_shape=[jax.ShapeDtypeStruct((t, 1), jnp.int32)] * 2 + [jax.ShapeDtypeStruct((t, 1), F32)] * 2,
        compiler_params=_params("parallel"),
        name="router",
    )(h, g.reshape(1, d), wr)


def _route_plan(i1, i2, n_experts, tm):
    t = i1.shape[0]
    e = jnp.concatenate([i1, i2])
    onehot = (e[:, None] == jnp.arange(n_experts, dtype=jnp.int32)[None, :]).astype(jnp.int32)
    csum = jnp.cumsum(onehot, axis=0)
    rank = jnp.take_along_axis(csum, e[:, None], axis=1)[:, 0] - 1
    counts = csum[-1]
    padded = ((counts + tm - 1) // tm) * tm
    ends = jnp.cumsum(padded)
    starts = ends - padded
    pos = starts[e] + rank
    n_rows = TOP_K * t + n_experts * tm
    tok = jnp.arange(t, dtype=jnp.int32)
    row_token = jnp.zeros((n_rows,), jnp.int32).at[pos].set(jnp.concatenate([tok, tok]))
    tile_start = jnp.arange(n_rows // tm, dtype=jnp.int32) * tm
    grp = jnp.sum((tile_start[:, None] >= ends[None, :]).astype(jnp.int32), axis=1)
    in_use = grp < n_experts
    grp = jnp.minimum(grp, n_experts - 1)
    valid = (in_use & (tile_start < (starts + counts)[grp])).astype(jnp.int32)
    first = jnp.concatenate([jnp.ones((1,), jnp.int32), (grp[1:] != grp[:-1]).astype(jnp.int32)])
    return (grp, first, valid), row_token, pos[:t], pos[t:]


def _gather_norm_kernel(tok_ref, valid_ref, g_ref, h_hbm, o_ref, buf, sem, *, tm):
    i = pl.program_id(0)

    def row_copy(r):
        return pltpu.make_async_copy(h_hbm.at[pl.ds(tok_ref[i * tm + r], 1)], buf.at[pl.ds(r, 1)], sem)

    @pl.when(valid_ref[i] == 0)
    def _skip():
        o_ref[...] = jnp.zeros(o_ref.shape, o_ref.dtype)

    @pl.when(valid_ref[i] == 1)
    def _gather():
        def start(r, carry):
            row_copy(r).start()
            return carry
        lax.fori_loop(0, tm, start, 0)

        def wait(r, carry):
            row_copy(r).wait()
            return carry
        lax.fori_loop(0, tm, wait, 0)
        o_ref[...] = _rms(buf[...], g_ref[...]).astype(o_ref.dtype)


def _gather_norm(h, g, row_token, valid, tm):
    t, d = h.shape
    n_rows = row_token.shape[0]
    grid_spec = pltpu.PrefetchScalarGridSpec(
        num_scalar_prefetch=2,
        grid=(n_rows // tm,),
        in_specs=[pl.BlockSpec((1, d), lambda i, tok, v: (0, 0)), pl.BlockSpec(memory_space=pl.ANY)],
        out_specs=pl.BlockSpec((tm, d), lambda i, tok, v: (i, 0)),
        scratch_shapes=[pltpu.VMEM((tm, d), F32), pltpu.SemaphoreType.DMA(())],
    )
    return pl.pallas_call(
        functools.partial(_gather_norm_kernel, tm=tm),
        grid_spec=grid_spec,
        out_shape=jax.ShapeDtypeStruct((n_rows, d), BF16),
        compiler_params=_params("arbitrary"),
        name="moe_gather_norm",
    )(row_token, valid, g.reshape(1, d), h)


def _combine_kernel(p1_ref, p2_ref, h_ref, g1_ref, g2_ref, fn_ref, y_hbm, o_ref, buf, sem, *, tm):
    i = pl.program_id(0)

    def row_copy(r, which, p_ref):
        return pltpu.make_async_copy(y_hbm.at[pl.ds(p_ref[i * tm + r], 1)], buf.at[which, pl.ds(r, 1)], sem)

    def start(r, carry):
        row_copy(r, 0, p1_ref).start()
        row_copy(r, 1, p2_ref).start()
        return carry
    lax.fori_loop(0, tm, start, 0)

    def wait(r, carry):
        row_copy(r, 0, p1_ref).wait()
        row_copy(r, 1, p2_ref).wait()
        return carry
    lax.fori_loop(0, tm, wait, 0)
    h = h_ref[...] + (g1_ref[...] * buf[0] + g2_ref[...] * buf[1])
    o_ref[...] = _rms(h, fn_ref[...])


def _combine_norm(h, y, pos1, pos2, g1, g2, final_norm):
    t, d = h.shape
    tm = 256
    col = pl.BlockSpec((tm, 1), lambda i, a, b: (i, 0))
    row = pl.BlockSpec((tm, d), lambda i, a, b: (i, 0))
    grid_spec = pltpu.PrefetchScalarGridSpec(
        num_scalar_prefetch=2,
        grid=(t // tm,),
        in_specs=[row, col, col, pl.BlockSpec((1, d), lambda i, a, b: (0, 0)),
                  pl.BlockSpec(memory_space=pl.ANY)],
        out_specs=row,
        scratch_shapes=[pltpu.VMEM((TOP_K, tm, d), F32), pltpu.SemaphoreType.DMA(())],
    )
    return pl.pallas_call(
        functools.partial(_combine_kernel, tm=tm),
        grid_spec=grid_spec,
        out_shape=jax.ShapeDtypeStruct((t, d), F32),
        compiler_params=_params("arbitrary"),
        name="moe_combine_norm",
    )(pos1, pos2, h, g1, g2, final_norm.reshape(1, d), y)


TM_DENSE = 1024
TN_DENSE = 512
TN_SWIGLU = 256
TM_EXPERT = 512
TN_SWIGLU_EXPERT = 512
DOWN_K_SPLITS = 4


def _down_proj(a, w, plan, res, tm, tn):
    k = w.shape[1]
    splits = DOWN_K_SPLITS if k > 8192 else 1
    kb = k // splits
    for ki in range(splits):
        res = _matmul(a, (w,), plan, mode="residual", tm=tm, tn=tn, out_dtype=F32,
                      k_blk=kb, k_idx=ki, extras=(res,))
    return res


def kernel(x, positions, mix_norm, ffn_norm, dil_w_in, dil_w_out, moba_w_in, moba_w_out,
           ffn_w_gate, ffn_w_up, ffn_w_down, router_w, exp_w_gate, exp_w_up, exp_w_down,
           final_norm):
    batch, s, d = x.shape
    t = batch * s
    attn_w = dil_w_out.shape[1]
    h = x.reshape(t, d)
    rope = _rope_tables(positions)
    dense = _dense_plan(t // TM_DENSE)

    hn = _rmsnorm(h, mix_norm[0])
    proj = _matmul(hn, (dil_w_in,), dense, mode="rope", tm=TM_DENSE, tn=TN_DENSE, out_dtype=BF16,
                   extras=rope, qkv_width=attn_w)
    o = _dilated_attention(proj, batch, attn_w)
    h = _matmul(o, (dil_w_out,), dense, mode="residual", tm=TM_DENSE, tn=TN_DENSE, out_dtype=F32,
                extras=(h,))
    hn = _rmsnorm(h, ffn_norm[0])
    a = _matmul(hn, (ffn_w_gate, ffn_w_up), dense, mode="swiglu", tm=TM_DENSE, tn=TN_SWIGLU,
                out_dtype=BF16)
    h = _down_proj(a, ffn_w_down, dense, h, TM_DENSE, TN_DENSE)

    hn = _rmsnorm(h, mix_norm[1])
    proj = _matmul(hn, (moba_w_in,), dense, mode="rope", tm=TM_DENSE, tn=TN_DENSE, out_dtype=BF16,
                   extras=rope, qkv_width=attn_w)
    o = _moba_attention(proj, batch, attn_w)
    h = _matmul(o, (moba_w_out,), dense, mode="residual", tm=TM_DENSE, tn=TN_DENSE, out_dtype=F32,
                extras=(h,))

    n_experts = router_w.shape[-1]
    i1, i2, g1, g2 = _router(h, ffn_norm[1], router_w[0])
    plan, row_token, pos1, pos2 = _route_plan(i1[:, 0], i2[:, 0], n_experts, TM_EXPERT)
    xs = _gather_norm(h, ffn_norm[1], row_token, plan[2], TM_EXPERT)
    a = _matmul(xs, (exp_w_gate[0], exp_w_up[0]), plan, mode="swiglu", tm=TM_EXPERT,
                tn=TN_SWIGLU_EXPERT, out_dtype=BF16)
    y = _matmul(a, (exp_w_down[0],), plan, mode="plain", tm=TM_EXPERT, tn=TN_DENSE, out_dtype=F32)
    out = _combine_norm(h, y, pos1, pos2, g1, g2, final_norm)
    return out.reshape(batch, s, d)
```

```python
import functools

import jax
import jax.numpy as jnp
from jax import lax
from jax.experimental import pallas as pl
from jax.experimental.pallas import tpu as pltpu

HEAD_DIM = 128
ROPE_DIM = HEAD_DIM // 4
ROPE_HALF = ROPE_DIM // 2
ROPE_THETA = 500000.0
SCALE = HEAD_DIM ** -0.5
NORM_EPS = 1e-5
TINY = 1e-30
DILATED_PAIRS = ((128, 1), (512, 4), (2048, 16))
MOBA_BLOCK = 256
MOBA_TOPK = 3
TOP_K = 2

LANES = 128
SUBLANES = 8
VMEM_LIMIT_BYTES = 56 * 2 ** 20

NEG_INF = float("-inf")
F32 = jnp.float32
BF16 = jnp.bfloat16

_NT_DIMS = (((1,), (1,)), ((), ()))


def _params(*sem):
    return pltpu.CompilerParams(dimension_semantics=sem, vmem_limit_bytes=VMEM_LIMIT_BYTES)


def _rope_table_kernel(pos_ref, invf_ref, c_ref, s1_ref, s2_ref):
    ang = pos_ref[...] * invf_ref[...]
    lane = lax.broadcasted_iota(jnp.int32, ang.shape, 1)
    c = jnp.cos(ang)
    s = jnp.sin(ang)
    c_ref[...] = jnp.where(lane < ROPE_DIM, c, 1.0)
    s1_ref[...] = jnp.where(lane < ROPE_HALF, -s, 0.0)
    s2_ref[...] = jnp.where(lane < ROPE_HALF, 0.0, jnp.where(lane < ROPE_DIM, s, 0.0))


def _rope_tables(positions):
    t = positions.size
    tm = 1024
    pos = positions.reshape(t, 1).astype(F32)
    inv_freq = jnp.power(ROPE_THETA, -jnp.arange(ROPE_HALF, dtype=F32) / ROPE_HALF)
    invf = jnp.zeros((1, LANES), F32).at[0, :ROPE_DIM].set(jnp.tile(inv_freq, 2))
    tab = jax.ShapeDtypeStruct((t, LANES), F32)
    spec = pl.BlockSpec((tm, LANES), lambda i: (i, 0))
    return pl.pallas_call(
        _rope_table_kernel,
        grid=(t // tm,),
        in_specs=[pl.BlockSpec((tm, 1), lambda i: (i, 0)), pl.BlockSpec((1, LANES), lambda i: (0, 0))],
        out_specs=[spec, spec, spec],
        out_shape=[tab, tab, tab],
        compiler_params=_params("parallel"),
        name="rope_tables",
    )(pos, invf)


def _rms(x, g):
    ms = jnp.mean(x * x, axis=-1, keepdims=True)
    return x * lax.rsqrt(ms + NORM_EPS) * g


def _rmsnorm_kernel(x_ref, g_ref, o_ref):
    o_ref[...] = _rms(x_ref[...], g_ref[...]).astype(o_ref.dtype)


def _rmsnorm(x, g, out_dtype=BF16):
    t, d = x.shape
    tm = 512
    return pl.pallas_call(
        _rmsnorm_kernel,
        grid=(t // tm,),
        in_specs=[pl.BlockSpec((tm, d), lambda i: (i, 0)), pl.BlockSpec((1, d), lambda i: (0, 0))],
        out_specs=pl.BlockSpec((tm, d), lambda i: (i, 0)),
        out_shape=jax.ShapeDtypeStruct((t, d), out_dtype),
        compiler_params=_params("parallel"),
        name="rmsnorm",
    )(x, g.reshape(1, d))


CAST_ROWS = 256
MM_ROW_CHUNK = 256


def _cast_weight(w_ref, wb_ref):
    def body(c, carry):
        r = pl.multiple_of(c * CAST_ROWS, CAST_ROWS)
        wb_ref[pl.ds(r, CAST_ROWS), :] = w_ref[0, pl.ds(r, CAST_ROWS), :].astype(wb_ref.dtype)
        return carry
    lax.fori_loop(0, wb_ref.shape[0] // CAST_ROWS, body, 0)


def _rope_rotate(acc, c, s1, s2):
    return (acc * c + pltpu.roll(acc, LANES - ROPE_HALF, 1) * s1
            + pltpu.roll(acc, ROPE_HALF, 1) * s2)


def _mm_kernel(grp_ref, first_ref, valid_ref, x_ref, *rest, mode, qkv_width):
    del grp_ref
    j = pl.program_id(0)
    i = pl.program_id(1)
    if mode == "swiglu":
        wg_ref, wu_ref, o_ref, wgb, wub = rest
        weights = ((wg_ref, wgb), (wu_ref, wub))
    elif mode == "residual":
        w_ref, res_ref, o_ref, wb = rest
        weights = ((w_ref, wb),)
    elif mode == "rope":
        w_ref, c_ref, s1_ref, s2_ref, o_ref, wb = rest
        weights = ((w_ref, wb),)
    else:
        w_ref, o_ref, wb = rest
        weights = ((w_ref, wb),)

    @pl.when(first_ref[i] == 1)
    def _cast():
        for w, b in weights:
            _cast_weight(w, b)

    @pl.when(valid_ref[i] == 0)
    def _skip():
        o_ref[...] = jnp.zeros(o_ref.shape, o_ref.dtype)

    @pl.when(valid_ref[i] == 1)
    def _compute():
        tm, tn = o_ref.shape
        for r0 in range(0, tm, MM_ROW_CHUNK):
            rows = slice(r0, r0 + MM_ROW_CHUNK)
            x = x_ref[rows, :]
            if mode == "swiglu":
                g = jnp.dot(x, wgb[...], preferred_element_type=F32)
                u = jnp.dot(x, wub[...], preferred_element_type=F32)
                o_ref[rows, :] = (g * (1.0 / (1.0 + jnp.exp(-g))) * u).astype(o_ref.dtype)
                continue
            acc = jnp.dot(x, wb[...], preferred_element_type=F32)
            if mode == "residual":
                o_ref[rows, :] = res_ref[rows, :] + acc
            elif mode == "rope":
                is_v = (((j * tn) // qkv_width) % 3 == 2).astype(jnp.int32)
                keep = jnp.broadcast_to(is_v, (MM_ROW_CHUNK, HEAD_DIM)) > 0
                c, s1, s2 = c_ref[rows, :], s1_ref[rows, :], s2_ref[rows, :]
                for hh in range(tn // HEAD_DIM):
                    cs = slice(hh * HEAD_DIM, (hh + 1) * HEAD_DIM)
                    a = acc[:, cs]
                    o_ref[rows, cs] = jnp.where(keep, a, _rope_rotate(a, c, s1, s2)).astype(o_ref.dtype)
            else:
                o_ref[rows, :] = acc.astype(o_ref.dtype)


def _dense_plan(n_tiles):
    grp = jnp.zeros((n_tiles,), jnp.int32)
    first = jnp.zeros((n_tiles,), jnp.int32).at[0].set(1)
    valid = jnp.ones((n_tiles,), jnp.int32)
    return grp, first, valid


def _matmul(x, ws, plan, *, mode, tm, tn, out_dtype, k_blk=None, k_idx=0, extras=(), qkv_width=0):
    m = x.shape[0]
    _, k, n = ws[0].shape
    kb = k if k_blk is None else k_blk
    grp, first, valid = plan
    x_spec = pl.BlockSpec((tm, kb), lambda j, i, g, f, v: (i, k_idx))
    w_spec = pl.BlockSpec((1, kb, tn), lambda j, i, g, f, v: (g[i], k_idx, j))
    o_spec = pl.BlockSpec((tm, tn), lambda j, i, g, f, v: (i, j))
    in_specs = [x_spec] + [w_spec] * len(ws)
    if mode == "residual":
        in_specs.append(o_spec)
    elif mode == "rope":
        in_specs += [pl.BlockSpec((tm, LANES), lambda j, i, g, f, v: (i, 0))] * 3
    grid_spec = pltpu.PrefetchScalarGridSpec(
        num_scalar_prefetch=3,
        grid=(n // tn, m // tm),
        in_specs=in_specs,
        out_specs=o_spec,
        scratch_shapes=[pltpu.VMEM((kb, tn), BF16)] * len(ws),
    )
    return pl.pallas_call(
        functools.partial(_mm_kernel, mode=mode, qkv_width=qkv_width),
        grid_spec=grid_spec,
        out_shape=jax.ShapeDtypeStruct((m, n), out_dtype),
        compiler_params=_params("arbitrary", "arbitrary"),
        name="mm_" + mode,
    )(grp, first, valid, x, *ws, *extras)


SPAN = 128
DIL_BLOCK_UNROLL = 16
DIL_VT_UNROLL = 8


def _dil_kernel(*refs, seq):
    qkv = refs[:9]
    o_ref = refs[9]
    stage, qp, kp, vt2, og, lg, o_run, l_run = refs[10:]
    n_blocks = seq // SPAN

    key2 = lax.broadcasted_iota(jnp.int32, (2 * SPAN, SPAN), 0)
    qry2 = lax.broadcasted_iota(jnp.int32, (2 * SPAN, SPAN), 1)
    in_cur = jnp.where(key2 >= SPAN, jnp.where(key2 - SPAN <= qry2, 0.0, NEG_INF), NEG_INF)
    bias_first = in_cur
    bias_both = jnp.where(key2 < SPAN, jnp.where(key2 >= qry2, 0.0, NEG_INF), in_cur)

    kp[0:SPAN, :] = jnp.zeros((SPAN, HEAD_DIM), BF16)
    vt2[0, :, 0:SPAN] = jnp.zeros((HEAD_DIM, SPAN), BF16)

    for g, (window, r) in enumerate(DILATED_PAIRS):
        q_ref, k_ref, v_ref = qkv[3 * g:3 * g + 3]
        sub_len = seq // r
        nblk = sub_len // SPAN
        last = g == len(DILATED_PAIRS) - 1

        def regroup(src_ref, dst_ref, off, r=r, sub_len=sub_len):
            if r == 1:
                dst_ref[off:off + seq, :] = src_ref[0].astype(dst_ref.dtype)
                return
            stage[...] = src_ref[0].astype(F32)
            for rr in range(r):
                dst_ref[off + rr * sub_len:off + (rr + 1) * sub_len, :] = (
                    stage[pl.ds(rr, sub_len, stride=r), :].astype(dst_ref.dtype))

        regroup(q_ref, qp, 0)
        regroup(k_ref, kp, SPAN)
        regroup(v_ref, og, 0)

        def vt_body(i, carry):
            base = pl.multiple_of(i * SPAN, SPAN)
            tr = og[pl.ds(base, SPAN), :].T.astype(BF16)
            vt2[i, :, SPAN:2 * SPAN] = tr
            vt2[i + 1, :, 0:SPAN] = tr
            return carry
        lax.fori_loop(0, n_blocks, vt_body, 0, unroll=DIL_VT_UNROLL)

        out_o, out_l = (o_run, l_run) if g == 0 else (og, lg)

        def blk_body(i, carry, nblk=nblk, out_o=out_o, out_l=out_l):
            base = pl.multiple_of(i * SPAN, SPAN)
            q = qp[pl.ds(base, SPAN), :]
            has_prev = jnp.broadcast_to((lax.rem(i, nblk) > 0).astype(jnp.int32), (2 * SPAN, SPAN)) > 0
            s = lax.dot_general(kp[pl.ds(base, 2 * SPAN), :], q, _NT_DIMS,
                                preferred_element_type=F32) * SCALE + jnp.where(has_prev, bias_both, bias_first)
            m = jnp.max(s, axis=0, keepdims=True)
            p = jnp.exp(s - m)
            den = jnp.sum(p, axis=0, keepdims=True)
            o_t = jnp.dot(vt2[i], p.astype(BF16), preferred_element_type=F32)
            o_t = o_t / jnp.maximum(den, TINY)
            lse = m + jnp.log(den)
            out_o[pl.ds(base, SPAN), :] = o_t.T
            out_l[pl.ds(base, SPAN), :] = jnp.broadcast_to(lse, (SPAN, SPAN)).T
            return carry
        lax.fori_loop(0, n_blocks, blk_body, 0, unroll=DIL_BLOCK_UNROLL)

        if g > 0:
            for rr in range(r):
                for c in range(sub_len // SPAN):
                    tok = pl.ds(rr + c * SPAN * r, SPAN, stride=r)
                    rows = pl.ds(rr * sub_len + c * SPAN, SPAN)
                    lp = l_run[tok, :]
                    ln = lg[rows, :]
                    mx = jnp.maximum(lp, ln)
                    wp = jnp.exp(lp - mx)
                    wn = jnp.exp(ln - mx)
                    tot = wp + wn
                    o_run[tok, :] = (o_run[tok, :] * wp + og[rows, :] * wn) / tot
                    if not last:
                        l_run[tok, :] = mx + jnp.log(tot)

    o_ref[0] = o_run[...].astype(o_ref.dtype)


def _dilated_attention(proj, batch, attn_w):
    t, e = proj.shape
    s = t // batch
    n_heads = attn_w // HEAD_DIM
    for window, r in DILATED_PAIRS:
        assert window // r == SPAN and s % (r * SPAN) == 0
    pv = proj.reshape(batch, s, e)

    def spec(g, c):
        return pl.BlockSpec((1, s, HEAD_DIM), lambda b, h: (b, 0, (g * 3 + c) * n_heads + h))

    seq_f32 = pltpu.VMEM((s, HEAD_DIM), F32)
    seq_bf16 = pltpu.VMEM((s, HEAD_DIM), BF16)
    out = pl.pallas_call(
        functools.partial(_dil_kernel, seq=s),
        grid=(batch, n_heads),
        in_specs=[spec(g, c) for g in range(len(DILATED_PAIRS)) for c in range(3)],
        out_specs=pl.BlockSpec((1, s, HEAD_DIM), lambda b, h: (b, 0, h)),
        out_shape=jax.ShapeDtypeStruct((batch, s, attn_w), BF16),
        scratch_shapes=[seq_f32, seq_bf16, pltpu.VMEM((s + SPAN, HEAD_DIM), BF16),
                        pltpu.VMEM((s // SPAN + 1, HEAD_DIM, 2 * SPAN), BF16),
                        seq_f32, seq_f32, seq_f32, seq_f32],
        compiler_params=_params("parallel", "parallel"),
        name="dilated_attn",
    )(*([pv] * 9))
    return out.reshape(t, attn_w)


MOBA_HEADS_PER_STEP = 2
MOBA_BLOCKS_PER_ITER = 4


def _moba_kernel(q_ref, k_ref, v_ref, o_ref, vt, kmean, sel_ref, acc_ref, *, seq):
    blk = MOBA_BLOCK
    nb = seq // blk
    nbp = kmean.shape[1]
    heads = range(MOBA_HEADS_PER_STEP)
    per_it = MOBA_BLOCKS_PER_ITER
    kmean[...] = jnp.zeros(kmean.shape, F32)
    sel_ref[...] = jnp.full(sel_ref.shape, NEG_INF, F32)

    def head_cols(h):
        return slice(h * HEAD_DIM, (h + 1) * HEAD_DIM)

    def pre(jb, carry):
        r0 = pl.multiple_of(jb * blk, blk)
        for h in heads:
            vt[h, jb] = v_ref[0, pl.ds(r0, blk), head_cols(h)].astype(F32).T.astype(BF16)
            kb = k_ref[0, pl.ds(r0, blk), head_cols(h)].astype(F32)
            kmean[h, pl.ds(jb, 1), :] = jnp.sum(kb, axis=0, keepdims=True) * (1.0 / blk)
        return carry
    lax.fori_loop(0, nb, pre, 0)

    kblk = lax.broadcasted_iota(jnp.int32, (nbp, blk), 0)
    key = lax.broadcasted_iota(jnp.int32, (blk, blk), 0)
    qry = lax.broadcasted_iota(jnp.int32, (blk, blk), 1)
    bias_causal = jnp.where(key <= qry, 0.0, NEG_INF)

    def q_body(qi, carry):
        r0 = pl.multiple_of(qi * blk, blk)
        qs, ms, ls = [], [], []
        for h in heads:
            q = q_ref[0, pl.ds(r0, blk), head_cols(h)]
            gate = lax.dot_general(kmean[h], q.astype(F32), _NT_DIMS,
                                   preferred_element_type=F32, precision=lax.Precision.HIGHEST)
            gate = jnp.where(kblk < qi, gate, NEG_INF)
            beaten = jnp.zeros((nbp, blk), F32)
            for jp in range(nb):
                gj = gate[jp:jp + 1, :]
                tie = jnp.where(kblk > jp, 1.0, 0.0)
                beaten = beaten + jnp.where(gj > gate, 1.0, jnp.where(gj == gate, tie, 0.0))
            sel_ref[h, 0:nbp, :] = jnp.where(beaten < MOBA_TOPK, jnp.where(kblk < qi, 0.0, NEG_INF), NEG_INF)

            s = lax.dot_general(k_ref[0, pl.ds(r0, blk), head_cols(h)], q, _NT_DIMS,
                                preferred_element_type=F32) * SCALE + bias_causal
            m = jnp.max(s, axis=0, keepdims=True)
            p = jnp.exp(s - m)
            acc_ref[h] = jnp.dot(vt[h, qi], p.astype(BF16), preferred_element_type=F32)
            qs.append(q)
            ms.append(m)
            ls.append(jnp.sum(p, axis=0, keepdims=True))

        def past(it, st):
            ms, ls = st
            new_m, new_l = [], []
            for h in heads:
                m, l = ms[h], ls[h]
                blocks = [it * per_it + u for u in range(per_it)]
                ss = []
                for jb in blocks:
                    c0 = pl.multiple_of(jnp.minimum(jb, nb - 1) * blk, blk)
                    ss.append(lax.dot_general(k_ref[0, pl.ds(c0, blk), head_cols(h)], qs[h], _NT_DIMS,
                                              preferred_element_type=F32) * SCALE + sel_ref[h, pl.ds(jb, 1), :])
                m_new = m
                for s in ss:
                    m_new = jnp.maximum(m_new, jnp.max(s, axis=0, keepdims=True))
                alpha = jnp.exp(m - m_new)
                l = alpha * l
                acc = alpha * acc_ref[h]
                for jb, s in zip(blocks, ss):
                    p = jnp.exp(s - m_new)
                    l = l + jnp.sum(p, axis=0, keepdims=True)
                    acc = acc + jnp.dot(vt[h, jnp.minimum(jb, nb - 1)], p.astype(BF16),
                                        preferred_element_type=F32)
                acc_ref[h] = acc
                new_m.append(m_new)
                new_l.append(l)
            return tuple(new_m), tuple(new_l)

        ms, ls = lax.fori_loop(0, (qi + per_it - 1) // per_it, past, (tuple(ms), tuple(ls)))
        for h in heads:
            o_ref[0, pl.ds(r0, blk), head_cols(h)] = (
                (acc_ref[h] / jnp.maximum(ls[h], TINY)).T.astype(o_ref.dtype))
        return carry

    lax.fori_loop(0, nb, q_body, 0)


def _moba_attention(proj, batch, attn_w):
    t, e = proj.shape
    s = t // batch
    assert s % MOBA_BLOCK == 0
    nb = s // MOBA_BLOCK
    nbp = -(-nb // SUBLANES) * SUBLANES
    n_heads = attn_w // HEAD_DIM
    hp = MOBA_HEADS_PER_STEP
    assert n_heads % hp == 0
    w = hp * HEAD_DIM
    pv = proj.reshape(batch, s, e)

    def spec(c):
        return pl.BlockSpec((1, s, w), lambda b, h: (b, 0, c * (n_heads // hp) + h))

    out = pl.pallas_call(
        functools.partial(_moba_kernel, seq=s),
        grid=(batch, n_heads // hp),
        in_specs=[spec(0), spec(1), spec(2)],
        out_specs=pl.BlockSpec((1, s, w), lambda b, h: (b, 0, h)),
        out_shape=jax.ShapeDtypeStruct((batch, s, attn_w), BF16),
        scratch_shapes=[pltpu.VMEM((hp, nb, HEAD_DIM, MOBA_BLOCK), BF16),
                        pltpu.VMEM((hp, nbp, HEAD_DIM), F32),
                        pltpu.VMEM((hp, nbp + MOBA_BLOCKS_PER_ITER, MOBA_BLOCK), F32),
                        pltpu.VMEM((hp, HEAD_DIM, MOBA_BLOCK), F32)],
        compiler_params=_params("parallel", "parallel"),
        name="moba_attn",
    )(pv, pv, pv)
    return out.reshape(t, attn_w)


def _router_kernel(x_ref, g_ref, wr_ref, i1_ref, i2_ref, g1_ref, g2_ref, *, n_experts):
    y = _rms(x_ref[...], g_ref[...])
    lg = jnp.dot(y, wr_ref[...], preferred_element_type=F32, precision=lax.Precision.HIGHEST)
    lane = lax.broadcasted_iota(jnp.int32, lg.shape, 1).astype(F32)
    lg = jnp.where(lane < n_experts, lg, NEG_INF)
    m1 = jnp.max(lg, axis=1, keepdims=True)
    i1 = jnp.min(jnp.where(lg == m1, lane, float(LANES)), axis=1, keepdims=True)
    lg2 = jnp.where(lane == i1, NEG_INF, lg)
    m2 = jnp.max(lg2, axis=1, keepdims=True)
    i2 = jnp.min(jnp.where(lg2 == m2, lane, float(LANES)), axis=1, keepdims=True)
    e = jnp.exp(m2 - m1)
    i1_ref[...] = i1.astype(jnp.int32)
    i2_ref[...] = i2.astype(jnp.int32)
    g1_ref[...] = 1.0 / (1.0 + e)
    g2_ref[...] = e / (1.0 + e)


def _router(h, g, w_router):
    t, d = h.shape
    n_experts = w_router.shape[1]
    assert n_experts <= LANES
    tm = 256
    wr = jnp.zeros((d, LANES), F32).at[:, :n_experts].set(w_router)
    col = pl.BlockSpec((tm, 1), lambda i: (i, 0))
    return pl.pallas_call(
        functools.partial(_router_kernel, n_experts=n_experts),
        grid=(t // tm,),
        in_specs=[pl.BlockSpec((tm, d), lambda i: (i, 0)), pl.BlockSpec((1, d), lambda i: (0, 0)),
                  pl.BlockSpec((d, LANES), lambda i: (0, 0))],
        out_specs=[col, col, col, col],
        out_shape=[jax.ShapeDtypeStruct((t, 1), jnp.int32)] * 2 + [jax.ShapeDtypeStruct((t, 1), F32)] * 2,
        compiler_params=_params("parallel"),
        name="router",
    )(h, g.reshape(1, d), wr)


def _route_plan(i1, i2, n_experts, tm):
    t = i1.shape[0]
    e = jnp.concatenate([i1, i2])
    onehot = (e[:, None] == jnp.arange(n_experts, dtype=jnp.int32)[None, :]).astype(jnp.int32)
    csum = jnp.cumsum(onehot, axis=0)
    rank = jnp.take_along_axis(csum, e[:, None], axis=1)[:, 0] - 1
    counts = csum[-1]
    padded = ((counts + tm - 1) // tm) * tm
    ends = jnp.cumsum(padded)
    starts = ends - padded
    pos = starts[e] + rank
    n_rows = TOP_K * t + n_experts * tm
    tok = jnp.arange(t, dtype=jnp.int32)
    row_token = jnp.zeros((n_rows,), jnp.int32).at[pos].set(jnp.concatenate([tok, tok]))
    tile_start = jnp.arange(n_rows // tm, dtype=jnp.int32) * tm
    grp = jnp.sum((tile_start[:, None] >= ends[None, :]).astype(jnp.int32), axis=1)
    in_use = grp < n_experts
    grp = jnp.minimum(grp, n_experts - 1)
    valid = (in_use & (tile_start < (starts + counts)[grp])).astype(jnp.int32)
    first = jnp.concatenate([jnp.ones((1,), jnp.int32), (grp[1:] != grp[:-1]).astype(jnp.int32)])
    return (grp, first, valid), row_token, pos[:t], pos[t:]


def _gather_norm_kernel(tok_ref, valid_ref, g_ref, h_hbm, o_ref, buf, sem, *, tm):
    i = pl.program_id(0)

    def row_copy(r):
        return pltpu.make_async_copy(h_hbm.at[pl.ds(tok_ref[i * tm + r], 1)], buf.at[pl.ds(r, 1)], sem)

    @pl.when(valid_ref[i] == 0)
    def _skip():
        o_ref[...] = jnp.zeros(o_ref.shape, o_ref.dtype)

    @pl.when(valid_ref[i] == 1)
    def _gather():
        def start(r, carry):
            row_copy(r).start()
            return carry
        lax.fori_loop(0, tm, start, 0)

        def wait(r, carry):
            row_copy(r).wait()
            return carry
        lax.fori_loop(0, tm, wait, 0)
        o_ref[...] = _rms(buf[...], g_ref[...]).astype(o_ref.dtype)


def _gather_norm(h, g, row_token, valid, tm):
    t, d = h.shape
    n_rows = row_token.shape[0]
    grid_spec = pltpu.PrefetchScalarGridSpec(
        num_scalar_prefetch=2,
        grid=(n_rows // tm,),
        in_specs=[pl.BlockSpec((1, d), lambda i, tok, v: (0, 0)), pl.BlockSpec(memory_space=pl.ANY)],
        out_specs=pl.BlockSpec((tm, d), lambda i, tok, v: (i, 0)),
        scratch_shapes=[pltpu.VMEM((tm, d), F32), pltpu.SemaphoreType.DMA(())],
    )
    return pl.pallas_call(
        functools.partial(_gather_norm_kernel, tm=tm),
        grid_spec=grid_spec,
        out_shape=jax.ShapeDtypeStruct((n_rows, d), BF16),
        compiler_params=_params("arbitrary"),
        name="moe_gather_norm",
    )(row_token, valid, g.reshape(1, d), h)


def _combine_kernel(p1_ref, p2_ref, h_ref, g1_ref, g2_ref, fn_ref, y_hbm, o_ref, buf, sem, *, tm):
    i = pl.program_id(0)

    def row_copy(r, which, p_ref):
        return pltpu.make_async_copy(y_hbm.at[pl.ds(p_ref[i * tm + r], 1)], buf.at[which, pl.ds(r, 1)], sem)

    def start(r, carry):
        row_copy(r, 0, p1_ref).start()
        row_copy(r, 1, p2_ref).start()
        return carry
    lax.fori_loop(0, tm, start, 0)

    def wait(r, carry):
        row_copy(r, 0, p1_ref).wait()
        row_copy(r, 1, p2_ref).wait()
        return carry
    lax.fori_loop(0, tm, wait, 0)
    h = h_ref[...] + (g1_ref[...] * buf[0] + g2_ref[...] * buf[1])
    o_ref[...] = _rms(h, fn_ref[...])


def _combine_norm(h, y, pos1, pos2, g1, g2, final_norm):
    t, d = h.shape
    tm = 256
    col = pl.BlockSpec((tm, 1), lambda i, a, b: (i, 0))
    row = pl.BlockSpec((tm, d), lambda i, a, b: (i, 0))
    grid_spec = pltpu.PrefetchScalarGridSpec(
        num_scalar_prefetch=2,
        grid=(t // tm,),
        in_specs=[row, col, col, pl.BlockSpec((1, d), lambda i, a, b: (0, 0)),
                  pl.BlockSpec(memory_space=pl.ANY)],
        out_specs=row,
        scratch_shapes=[pltpu.VMEM((TOP_K, tm, d), F32), pltpu.SemaphoreType.DMA(())],
    )
    return pl.pallas_call(
        functools.partial(_combine_kernel, tm=tm),
        grid_spec=grid_spec,
        out_shape=jax.ShapeDtypeStruct((t, d), F32),
        compiler_params=_params("arbitrary"),
        name="moe_combine_norm",
    )(pos1, pos2, h, g1, g2, final_norm.reshape(1, d), y)


TM_DENSE = 1024
TN_DENSE = 512
TN_SWIGLU = 256
TM_EXPERT = 512
TN_SWIGLU_EXPERT = 512
DOWN_K_SPLITS = 4


def _down_proj(a, w, plan, res, tm, tn):
    k = w.shape[1]
    splits = DOWN_K_SPLITS if k > 8192 else 1
    kb = k // splits
    for ki in range(splits):
        res = _matmul(a, (w,), plan, mode="residual", tm=tm, tn=tn, out_dtype=F32,
                      k_blk=kb, k_idx=ki, extras=(res,))
    return res


def kernel(x, positions, mix_norm, ffn_norm, dil_w_in, dil_w_out, moba_w_in, moba_w_out,
           ffn_w_gate, ffn_w_up, ffn_w_down, router_w, exp_w_gate, exp_w_up, exp_w_down,
           final_norm):
    batch, s, d = x.shape
    t = batch * s
    attn_w = dil_w_out.shape[1]
    h = x.reshape(t, d)
    rope = _rope_tables(positions)
    dense = _dense_plan(t // TM_DENSE)

    hn = _rmsnorm(h, mix_norm[0])
    proj = _matmul(hn, (dil_w_in,), dense, mode="rope", tm=TM_DENSE, tn=TN_DENSE, out_dtype=BF16,
                   extras=rope, qkv_width=attn_w)
    o = _dilated_attention(proj, batch, attn_w)
    h = _matmul(o, (dil_w_out,), dense, mode="residual", tm=TM_DENSE, tn=TN_DENSE, out_dtype=F32,
                extras=(h,))
    hn = _rmsnorm(h, ffn_norm[0])
    a = _matmul(hn, (ffn_w_gate, ffn_w_up), dense, mode="swiglu", tm=TM_DENSE, tn=TN_SWIGLU,
                out_dtype=BF16)
    h = _down_proj(a, ffn_w_down, dense, h, TM_DENSE, TN_DENSE)

    hn = _rmsnorm(h, mix_norm[1])
    proj = _matmul(hn, (moba_w_in,), dense, mode="rope", tm=TM_DENSE, tn=TN_DENSE, out_dtype=BF16,
                   extras=rope, qkv_width=attn_w)
    o = _moba_attention(proj, batch, attn_w)
    h = _matmul(o, (moba_w_out,), dense, mode="residual", tm=TM_DENSE, tn=TN_DENSE, out_dtype=F32,
                extras=(h,))

    n_experts = router_w.shape[-1]
    i1, i2, g1, g2 = _router(h, ffn_norm[1], router_w[0])
    plan, row_token, pos1, pos2 = _route_plan(i1[:, 0], i2[:, 0], n_experts, TM_EXPERT)
    xs = _gather_norm(h, ffn_norm[1], row_token, plan[2], TM_EXPERT)
    a = _matmul(xs, (exp_w_gate[0], exp_w_up[0]), plan, mode="swiglu", tm=TM_EXPERT,
                tn=TN_SWIGLU_EXPERT, out_dtype=BF16)
    y = _matmul(a, (exp_w_down[0],), plan, mode="plain", tm=TM_EXPERT, tn=TN_DENSE, out_dtype=F32)
    out = _combine_norm(h, y, pos1, pos2, g1, g2, final_norm)
    return out.reshape(batch, s, d)
```

```python
import functools

import jax
import jax.numpy as jnp
from jax import lax
from jax.experimental import pallas as pl
from jax.experimental.pallas import tpu as pltpu

HEAD_DIM = 128
ROPE_DIM = HEAD_DIM // 4
ROPE_HALF = ROPE_DIM // 2
ROPE_THETA = 500000.0
SCALE = HEAD_DIM ** -0.5
NORM_EPS = 1e-5
TINY = 1e-30
DILATED_PAIRS = ((128, 1), (512, 4), (2048, 16))
MOBA_BLOCK = 256
MOBA_TOPK = 3
TOP_K = 2

LANES = 128
SUBLANES = 8
VMEM_LIMIT_BYTES = 56 * 2 ** 20

NEG_INF = float("-inf")
F32 = jnp.float32
BF16 = jnp.bfloat16

_NT_DIMS = (((1,), (1,)), ((), ()))


def _params(*sem):
    return pltpu.CompilerParams(dimension_semantics=sem, vmem_limit_bytes=VMEM_LIMIT_BYTES)


def _rope_table_kernel(pos_ref, invf_ref, c_ref, s1_ref, s2_ref):
    ang = pos_ref[...] * invf_ref[...]
    lane = lax.broadcasted_iota(jnp.int32, ang.shape, 1)
    c = jnp.cos(ang)
    s = jnp.sin(ang)
    c_ref[...] = jnp.where(lane < ROPE_DIM, c, 1.0)
    s1_ref[...] = jnp.where(lane < ROPE_HALF, -s, 0.0)
    s2_ref[...] = jnp.where(lane < ROPE_HALF, 0.0, jnp.where(lane < ROPE_DIM, s, 0.0))


def _rope_tables(positions):
    t = positions.size
    tm = 1024
    pos = positions.reshape(t, 1).astype(F32)
    inv_freq = jnp.power(ROPE_THETA, -jnp.arange(ROPE_HALF, dtype=F32) / ROPE_HALF)
    invf = jnp.zeros((1, LANES), F32).at[0, :ROPE_DIM].set(jnp.tile(inv_freq, 2))
    tab = jax.ShapeDtypeStruct((t, LANES), F32)
    spec = pl.BlockSpec((tm, LANES), lambda i: (i, 0))
    return pl.pallas_call(
        _rope_table_kernel,
        grid=(t // tm,),
        in_specs=[pl.BlockSpec((tm, 1), lambda i: (i, 0)), pl.BlockSpec((1, LANES), lambda i: (0, 0))],
        out_specs=[spec, spec, spec],
        out_shape=[tab, tab, tab],
        compiler_params=_params("parallel"),
        name="rope_tables",
    )(pos, invf)


def _rms(x, g):
    ms = jnp.mean(x * x, axis=-1, keepdims=True)
    return x * lax.rsqrt(ms + NORM_EPS) * g


def _rmsnorm_kernel(x_ref, g_ref, o_ref):
    o_ref[...] = _rms(x_ref[...], g_ref[...]).astype(o_ref.dtype)


def _rmsnorm(x, g, out_dtype=BF16):
    t, d = x.shape
    tm = 512
    return pl.pallas_call(
        _rmsnorm_kernel,
        grid=(t // tm,),
        in_specs=[pl.BlockSpec((tm, d), lambda i: (i, 0)), pl.BlockSpec((1, d), lambda i: (0, 0))],
        out_specs=pl.BlockSpec((tm, d), lambda i: (i, 0)),
        out_shape=jax.ShapeDtypeStruct((t, d), out_dtype),
        compiler_params=_params("parallel"),
        name="rmsnorm",
    )(x, g.reshape(1, d))


CAST_ROWS = 256
MM_ROW_CHUNK = 256


def _cast_weight(w_ref, wb_ref):
    def body(c, carry):
        r = pl.multiple_of(c * CAST_ROWS, CAST_ROWS)
        wb_ref[pl.ds(r, CAST_ROWS), :] = w_ref[0, pl.ds(r, CAST_ROWS), :].astype(wb_ref.dtype)
        return carry
    lax.fori_loop(0, wb_ref.shape[0] // CAST_ROWS, body, 0)


def _rope_rotate(acc, c, s1, s2):
    return (acc * c + pltpu.roll(acc, LANES - ROPE_HALF, 1) * s1
            + pltpu.roll(acc, ROPE_HALF, 1) * s2)


def _mm_kernel(grp_ref, first_ref, nrows_ref, x_ref, *rest, mode, qkv_width, ragged):
    del grp_ref
    j = pl.program_id(0)
    i = pl.program_id(1)
    if mode == "swiglu":
        wg_ref, wu_ref, o_ref, wgb, wub = rest
        weights = ((wg_ref, wgb), (wu_ref, wub))
    elif mode == "residual":
        w_ref, res_ref, o_ref, wb = rest
        weights = ((w_ref, wb),)
    elif mode == "rope":
        w_ref, c_ref, s1_ref, s2_ref, o_ref, wb = rest
        weights = ((w_ref, wb),)
    else:
        w_ref, o_ref, wb = rest
        weights = ((w_ref, wb),)

    @pl.when(first_ref[i] == 1)
    def _cast():
        for w, b in weights:
            _cast_weight(w, b)

    tm, tn = o_ref.shape

    def chunk(r0):
        rows = slice(r0, r0 + MM_ROW_CHUNK)
        x = x_ref[rows, :]
        if mode == "swiglu":
            g = jnp.dot(x, wgb[...], preferred_element_type=F32)
            u = jnp.dot(x, wub[...], preferred_element_type=F32)
            o_ref[rows, :] = (g * (1.0 / (1.0 + jnp.exp(-g))) * u).astype(o_ref.dtype)
            return
        acc = jnp.dot(x, wb[...], preferred_element_type=F32)
        if mode == "residual":
            o_ref[rows, :] = res_ref[rows, :] + acc
        elif mode == "rope":
            is_v = (((j * tn) // qkv_width) % 3 == 2).astype(jnp.int32)
            keep = jnp.broadcast_to(is_v, (MM_ROW_CHUNK, HEAD_DIM)) > 0
            c, s1, s2 = c_ref[rows, :], s1_ref[rows, :], s2_ref[rows, :]
            for hh in range(tn // HEAD_DIM):
                cs = slice(hh * HEAD_DIM, (hh + 1) * HEAD_DIM)
                a = acc[:, cs]
                o_ref[rows, cs] = jnp.where(keep, a, _rope_rotate(a, c, s1, s2)).astype(o_ref.dtype)
        else:
            o_ref[rows, :] = acc.astype(o_ref.dtype)

    for r0 in range(0, tm, MM_ROW_CHUNK):
        if not ragged:
            chunk(r0)
            continue
        pl.when(r0 < nrows_ref[i])(functools.partial(chunk, r0))

        @pl.when(r0 >= nrows_ref[i])
        def _zero(r0=r0):
            o_ref[r0:r0 + MM_ROW_CHUNK, :] = jnp.zeros((MM_ROW_CHUNK, tn), o_ref.dtype)


def _dense_plan(n_tiles, tm):
    grp = jnp.zeros((n_tiles,), jnp.int32)
    first = jnp.zeros((n_tiles,), jnp.int32).at[0].set(1)
    nrows = jnp.full((n_tiles,), tm, jnp.int32)
    return grp, first, nrows


def _matmul(x, ws, plan, *, mode, tm, tn, out_dtype, k_blk=None, k_idx=0, extras=(), qkv_width=0,
            ragged=False):
    m = x.shape[0]
    _, k, n = ws[0].shape
    kb = k if k_blk is None else k_blk
    grp, first, valid = plan
    x_spec = pl.BlockSpec((tm, kb), lambda j, i, g, f, v: (i, k_idx))
    w_spec = pl.BlockSpec((1, kb, tn), lambda j, i, g, f, v: (g[i], k_idx, j))
    o_spec = pl.BlockSpec((tm, tn), lambda j, i, g, f, v: (i, j))
    in_specs = [x_spec] + [w_spec] * len(ws)
    if mode == "residual":
        in_specs.append(o_spec)
    elif mode == "rope":
        in_specs += [pl.BlockSpec((tm, LANES), lambda j, i, g, f, v: (i, 0))] * 3
    grid_spec = pltpu.PrefetchScalarGridSpec(
        num_scalar_prefetch=3,
        grid=(n // tn, m // tm),
        in_specs=in_specs,
        out_specs=o_spec,
        scratch_shapes=[pltpu.VMEM((kb, tn), BF16)] * len(ws),
    )
    return pl.pallas_call(
        functools.partial(_mm_kernel, mode=mode, qkv_width=qkv_width, ragged=ragged),
        grid_spec=grid_spec,
        out_shape=jax.ShapeDtypeStruct((m, n), out_dtype),
        compiler_params=_params("arbitrary", "arbitrary"),
        name="mm_" + mode,
    )(grp, first, valid, x, *ws, *extras)


SPAN = 128
DIL_BLOCK_UNROLL = 16
DIL_VT_UNROLL = 8


def _dil_kernel(*refs, seq):
    qkv = refs[:9]
    o_ref = refs[9]
    stage, qp, kp, vt2, og, lg, o_run, l_run = refs[10:]
    n_blocks = seq // SPAN

    key2 = lax.broadcasted_iota(jnp.int32, (2 * SPAN, SPAN), 0)
    qry2 = lax.broadcasted_iota(jnp.int32, (2 * SPAN, SPAN), 1)
    in_cur = jnp.where(key2 >= SPAN, jnp.where(key2 - SPAN <= qry2, 0.0, NEG_INF), NEG_INF)
    bias_first = in_cur
    bias_both = jnp.where(key2 < SPAN, jnp.where(key2 >= qry2, 0.0, NEG_INF), in_cur)

    kp[0:SPAN, :] = jnp.zeros((SPAN, HEAD_DIM), BF16)
    vt2[0, :, 0:SPAN] = jnp.zeros((HEAD_DIM, SPAN), BF16)

    for g, (window, r) in enumerate(DILATED_PAIRS):
        q_ref, k_ref, v_ref = qkv[3 * g:3 * g + 3]
        sub_len = seq // r
        nblk = sub_len // SPAN
        last = g == len(DILATED_PAIRS) - 1

        def regroup(src_ref, dst_ref, off, r=r, sub_len=sub_len):
            if r == 1:
                dst_ref[off:off + seq, :] = src_ref[0].astype(dst_ref.dtype)
                return
            stage[...] = src_ref[0].astype(F32)
            for rr in range(r):
                dst_ref[off + rr * sub_len:off + (rr + 1) * sub_len, :] = (
                    stage[pl.ds(rr, sub_len, stride=r), :].astype(dst_ref.dtype))

        regroup(q_ref, qp, 0)
        regroup(k_ref, kp, SPAN)
        regroup(v_ref, og, 0)

        def vt_body(i, carry):
            base = pl.multiple_of(i * SPAN, SPAN)
            tr = og[pl.ds(base, SPAN), :].T.astype(BF16)
            vt2[i, :, SPAN:2 * SPAN] = tr
            vt2[i + 1, :, 0:SPAN] = tr
            return carry
        lax.fori_loop(0, n_blocks, vt_body, 0, unroll=DIL_VT_UNROLL)

        out_o, out_l = (o_run, l_run) if g == 0 else (og, lg)

        def blk_body(i, carry, nblk=nblk, out_o=out_o, out_l=out_l):
            base = pl.multiple_of(i * SPAN, SPAN)
            q = qp[pl.ds(base, SPAN), :]
            has_prev = jnp.broadcast_to((lax.rem(i, nblk) > 0).astype(jnp.int32), (2 * SPAN, SPAN)) > 0
            s = lax.dot_general(kp[pl.ds(base, 2 * SPAN), :], q, _NT_DIMS,
                                preferred_element_type=F32) * SCALE + jnp.where(has_prev, bias_both, bias_first)
            m = jnp.max(s, axis=0, keepdims=True)
            p = jnp.exp(s - m)
            den = jnp.sum(p, axis=0, keepdims=True)
            o_t = jnp.dot(vt2[i], p.astype(BF16), preferred_element_type=F32)
            o_t = o_t / jnp.maximum(den, TINY)
            lse = m + jnp.log(den)
            out_o[pl.ds(base, SPAN), :] = o_t.T
            out_l[pl.ds(base, SPAN), :] = jnp.broadcast_to(lse, (SPAN, SPAN)).T
            return carry
        lax.fori_loop(0, n_blocks, blk_body, 0, unroll=DIL_BLOCK_UNROLL)

        if g > 0:
            for rr in range(r):
                for c in range(sub_len // SPAN):
                    tok = pl.ds(rr + c * SPAN * r, SPAN, stride=r)
                    rows = pl.ds(rr * sub_len + c * SPAN, SPAN)
                    lp = l_run[tok, :]
                    ln = lg[rows, :]
                    mx = jnp.maximum(lp, ln)
                    wp = jnp.exp(lp - mx)
                    wn = jnp.exp(ln - mx)
                    tot = wp + wn
                    o_run[tok, :] = (o_run[tok, :] * wp + og[rows, :] * wn) / tot
                    if not last:
                        l_run[tok, :] = mx + jnp.log(tot)

    o_ref[0] = o_run[...].astype(o_ref.dtype)


def _dilated_attention(proj, batch, attn_w):
    t, e = proj.shape
    s = t // batch
    n_heads = attn_w // HEAD_DIM
    for window, r in DILATED_PAIRS:
        assert window // r == SPAN and s % (r * SPAN) == 0
    pv = proj.reshape(batch, s, e)

    def spec(g, c):
        return pl.BlockSpec((1, s, HEAD_DIM), lambda b, h: (b, 0, (g * 3 + c) * n_heads + h))

    seq_f32 = pltpu.VMEM((s, HEAD_DIM), F32)
    seq_bf16 = pltpu.VMEM((s, HEAD_DIM), BF16)
    out = pl.pallas_call(
        functools.partial(_dil_kernel, seq=s),
        grid=(batch, n_heads),
        in_specs=[spec(g, c) for g in range(len(DILATED_PAIRS)) for c in range(3)],
        out_specs=pl.BlockSpec((1, s, HEAD_DIM), lambda b, h: (b, 0, h)),
        out_shape=jax.ShapeDtypeStruct((batch, s, attn_w), BF16),
        scratch_shapes=[seq_f32, seq_bf16, pltpu.VMEM((s + SPAN, HEAD_DIM), BF16),
                        pltpu.VMEM((s // SPAN + 1, HEAD_DIM, 2 * SPAN), BF16),
                        seq_f32, seq_f32, seq_f32, seq_f32],
        compiler_params=_params("parallel", "parallel"),
        name="dilated_attn",
    )(*([pv] * 9))
    return out.reshape(t, attn_w)


MOBA_HEADS_PER_STEP = 2
MOBA_BLOCKS_PER_ITER = 4


def _moba_kernel(q_ref, k_ref, v_ref, o_ref, vt, kmean, sel_ref, acc_ref, *, seq):
    blk = MOBA_BLOCK
    nb = seq // blk
    nbp = kmean.shape[1]
    heads = range(MOBA_HEADS_PER_STEP)
    per_it = MOBA_BLOCKS_PER_ITER
    kmean[...] = jnp.zeros(kmean.shape, F32)
    sel_ref[...] = jnp.full(sel_ref.shape, NEG_INF, F32)

    def head_cols(h):
        return slice(h * HEAD_DIM, (h + 1) * HEAD_DIM)

    def pre(jb, carry):
        r0 = pl.multiple_of(jb * blk, blk)
        for h in heads:
            vt[h, jb] = v_ref[0, pl.ds(r0, blk), head_cols(h)].astype(F32).T.astype(BF16)
            kb = k_ref[0, pl.ds(r0, blk), head_cols(h)].astype(F32)
            kmean[h, pl.ds(jb, 1), :] = jnp.sum(kb, axis=0, keepdims=True) * (1.0 / blk)
        return carry
    lax.fori_loop(0, nb, pre, 0)

    kblk = lax.broadcasted_iota(jnp.int32, (nbp, blk), 0)
    key = lax.broadcasted_iota(jnp.int32, (blk, blk), 0)
    qry = lax.broadcasted_iota(jnp.int32, (blk, blk), 1)
    bias_causal = jnp.where(key <= qry, 0.0, NEG_INF)

    def q_body(qi, carry):
        r0 = pl.multiple_of(qi * blk, blk)
        qs, ms, ls = [], [], []
        for h in heads:
            q = q_ref[0, pl.ds(r0, blk), head_cols(h)]
            gate = lax.dot_general(kmean[h], q.astype(F32), _NT_DIMS,
                                   preferred_element_type=F32, precision=lax.Precision.HIGHEST)
            gate = jnp.where(kblk < qi, gate, NEG_INF)
            beaten = jnp.zeros((nbp, blk), F32)
            for jp in range(nb):
                gj = gate[jp:jp + 1, :]
                tie = jnp.where(kblk > jp, 1.0, 0.0)
                beaten = beaten + jnp.where(gj > gate, 1.0, jnp.where(gj == gate, tie, 0.0))
            sel_ref[h, 0:nbp, :] = jnp.where(beaten < MOBA_TOPK, jnp.where(kblk < qi, 0.0, NEG_INF), NEG_INF)

            s = lax.dot_general(k_ref[0, pl.ds(r0, blk), head_cols(h)], q, _NT_DIMS,
                                preferred_element_type=F32) * SCALE + bias_causal
            m = jnp.max(s, axis=0, keepdims=True)
            p = jnp.exp(s - m)
            acc_ref[h] = jnp.dot(vt[h, qi], p.astype(BF16), preferred_element_type=F32)
            qs.append(q)
            ms.append(m)
            ls.append(jnp.sum(p, axis=0, keepdims=True))

        def past(it, st):
            ms, ls = st
            new_m, new_l = [], []
            for h in heads:
                m, l = ms[h], ls[h]
                blocks = [it * per_it + u for u in range(per_it)]
                ss = []
                for jb in blocks:
                    c0 = pl.multiple_of(jnp.minimum(jb, nb - 1) * blk, blk)
                    ss.append(lax.dot_general(k_ref[0, pl.ds(c0, blk), head_cols(h)], qs[h], _NT_DIMS,
                                              preferred_element_type=F32) * SCALE + sel_ref[h, pl.ds(jb, 1), :])
                m_new = m
                for s in ss:
                    m_new = jnp.maximum(m_new, jnp.max(s, axis=0, keepdims=True))
                alpha = jnp.exp(m - m_new)
                l = alpha * l
                acc = alpha * acc_ref[h]
                for jb, s in zip(blocks, ss):
                    p = jnp.exp(s - m_new)
                    l = l + jnp.sum(p, axis=0, keepdims=True)
                    acc = acc + jnp.dot(vt[h, jnp.minimum(jb, nb - 1)], p.astype(BF16),
                                        preferred_element_type=F32)
                acc_ref[h] = acc
                new_m.append(m_new)
                new_l.append(l)
            return tuple(new_m), tuple(new_l)

        ms, ls = lax.fori_loop(0, (qi + per_it - 1) // per_it, past, (tuple(ms), tuple(ls)))
        for h in heads:
            o_ref[0, pl.ds(r0, blk), head_cols(h)] = (
                (acc_ref[h] / jnp.maximum(ls[h], TINY)).T.astype(o_ref.dtype))
        return carry

    lax.fori_loop(0, nb, q_body, 0)


def _moba_attention(proj, batch, attn_w):
    t, e = proj.shape
    s = t // batch
    assert s % MOBA_BLOCK == 0
    nb = s // MOBA_BLOCK
    nbp = -(-nb // SUBLANES) * SUBLANES
    n_heads = attn_w // HEAD_DIM
    hp = MOBA_HEADS_PER_STEP
    assert n_heads % hp == 0
    w = hp * HEAD_DIM
    pv = proj.reshape(batch, s, e)

    def spec(c):
        return pl.BlockSpec((1, s, w), lambda b, h: (b, 0, c * (n_heads // hp) + h))

    out = pl.pallas_call(
        functools.partial(_moba_kernel, seq=s),
        grid=(batch, n_heads // hp),
        in_specs=[spec(0), spec(1), spec(2)],
        out_specs=pl.BlockSpec((1, s, w), lambda b, h: (b, 0, h)),
        out_shape=jax.ShapeDtypeStruct((batch, s, attn_w), BF16),
        scratch_shapes=[pltpu.VMEM((hp, nb, HEAD_DIM, MOBA_BLOCK), BF16),
                        pltpu.VMEM((hp, nbp, HEAD_DIM), F32),
                        pltpu.VMEM((hp, nbp + MOBA_BLOCKS_PER_ITER, MOBA_BLOCK), F32),
                        pltpu.VMEM((hp, HEAD_DIM, MOBA_BLOCK), F32)],
        compiler_params=_params("parallel", "parallel"),
        name="moba_attn",
    )(pv, pv, pv)
    return out.reshape(t, attn_w)


def _router_kernel(x_ref, g_ref, wr_ref, i1_ref, i2_ref, g1_ref, g2_ref, *, n_experts):
    y = _rms(x_ref[...], g_ref[...])
    lg = jnp.dot(y, wr_ref[...], preferred_element_type=F32, precision=lax.Precision.HIGHEST)
    lane = lax.broadcasted_iota(jnp.int32, lg.shape, 1).astype(F32)
    lg = jnp.where(lane < n_experts, lg, NEG_INF)
    m1 = jnp.max(lg, axis=1, keepdims=True)
    i1 = jnp.min(jnp.where(lg == m1, lane, float(LANES)), axis=1, keepdims=True)
    lg2 = jnp.where(lane == i1, NEG_INF, lg)
    m2 = jnp.max(lg2, axis=1, keepdims=True)
    i2 = jnp.min(jnp.where(lg2 == m2, lane, float(LANES)), axis=1, keepdims=True)
    e = jnp.exp(m2 - m1)
    i1_ref[...] = i1.astype(jnp.int32)
    i2_ref[...] = i2.astype(jnp.int32)
    g1_ref[...] = 1.0 / (1.0 + e)
    g2_ref[...] = e / (1.0 + e)


def _router(h, g, w_router):
    t, d = h.shape
    n_experts = w_router.shape[1]
    assert n_experts <= LANES
    tm = 256
    wr = jnp.zeros((d, LANES), F32).at[:, :n_experts].set(w_router)
    col = pl.BlockSpec((tm, 1), lambda i: (i, 0))
    return pl.pallas_call(
        functools.partial(_router_kernel, n_experts=n_experts),
        grid=(t // tm,),
        in_specs=[pl.BlockSpec((tm, d), lambda i: (i, 0)), pl.BlockSpec((1, d), lambda i: (0, 0)),
                  pl.BlockSpec((d, LANES), lambda i: (0, 0))],
        out_specs=[col, col, col, col],
        out_shape=[jax.ShapeDtypeStruct((t, 1), jnp.int32)] * 2 + [jax.ShapeDtypeStruct((t, 1), F32)] * 2,
        compiler_params=_params("parallel"),
        name="router",
    )(h, g.reshape(1, d), wr)


def _route_plan(i1, i2, n_experts, tm):
    t = i1.shape[0]
    e = jnp.concatenate([i1, i2])
    onehot = (e[:, None] == jnp.arange(n_experts, dtype=jnp.int32)[None, :]).astype(jnp.int32)
    csum = jnp.cumsum(onehot, axis=0)
    rank = jnp.take_along_axis(csum, e[:, None], axis=1)[:, 0] - 1
    counts = csum[-1]
    padded = ((counts + tm - 1) // tm) * tm
    ends = jnp.cumsum(padded)
    starts = ends - padded
    pos = starts[e] + rank
    n_rows = TOP_K * t + n_experts * tm
    tok = jnp.arange(t, dtype=jnp.int32)
    row_token = jnp.zeros((n_rows,), jnp.int32).at[pos].set(jnp.concatenate([tok, tok]))
    tile_start = jnp.arange(n_rows // tm, dtype=jnp.int32) * tm
    grp = jnp.sum((tile_start[:, None] >= ends[None, :]).astype(jnp.int32), axis=1)
    in_use = grp < n_experts
    grp = jnp.minimum(grp, n_experts - 1)
    nrows = jnp.where(in_use, jnp.clip((starts + counts)[grp] - tile_start, 0, tm), 0).astype(jnp.int32)
    first = jnp.concatenate([jnp.ones((1,), jnp.int32), (grp[1:] != grp[:-1]).astype(jnp.int32)])
    g_start = jnp.arange(n_rows // GATHER_ROWS, dtype=jnp.int32) * GATHER_ROWS
    g_valid = ((g_start % tm) < nrows[g_start // tm]).astype(jnp.int32)
    return (grp, first, nrows), row_token, g_valid, pos[:t], pos[t:]


GATHER_ROWS = 256
DMA_ISSUE_UNROLL = 8


def _gather_norm_kernel(tok_ref, valid_ref, g_ref, h_hbm, o_ref, buf, sem, *, tm):
    i = pl.program_id(0)

    def row_copy(r):
        return pltpu.make_async_copy(h_hbm.at[pl.ds(tok_ref[i * tm + r], 1)], buf.at[pl.ds(r, 1)], sem)

    @pl.when(valid_ref[i] == 0)
    def _skip():
        o_ref[...] = jnp.zeros(o_ref.shape, o_ref.dtype)

    @pl.when(valid_ref[i] == 1)
    def _gather():
        def start(r, carry):
            row_copy(r).start()
            return carry
        lax.fori_loop(0, tm, start, 0, unroll=DMA_ISSUE_UNROLL)

        def wait(r, carry):
            row_copy(r).wait()
            return carry
        lax.fori_loop(0, tm, wait, 0, unroll=DMA_ISSUE_UNROLL)
        o_ref[...] = _rms(buf[...], g_ref[...]).astype(o_ref.dtype)


def _gather_norm(h, g, row_token, valid, tm):
    t, d = h.shape
    n_rows = row_token.shape[0]
    grid_spec = pltpu.PrefetchScalarGridSpec(
        num_scalar_prefetch=2,
        grid=(n_rows // tm,),
        in_specs=[pl.BlockSpec((1, d), lambda i, tok, v: (0, 0)), pl.BlockSpec(memory_space=pl.ANY)],
        out_specs=pl.BlockSpec((tm, d), lambda i, tok, v: (i, 0)),
        scratch_shapes=[pltpu.VMEM((tm, d), F32), pltpu.SemaphoreType.DMA(())],
    )
    return pl.pallas_call(
        functools.partial(_gather_norm_kernel, tm=tm),
        grid_spec=grid_spec,
        out_shape=jax.ShapeDtypeStruct((n_rows, d), BF16),
        compiler_params=_params("arbitrary"),
        name="moe_gather_norm",
    )(row_token, valid, g.reshape(1, d), h)


def _combine_kernel(p1_ref, p2_ref, h_ref, g1_ref, g2_ref, fn_ref, y_hbm, o_ref, buf, sem, *, tm):
    i = pl.program_id(0)

    def row_copy(r, which, p_ref):
        return pltpu.make_async_copy(y_hbm.at[pl.ds(p_ref[i * tm + r], 1)], buf.at[which, pl.ds(r, 1)], sem)

    def start(r, carry):
        row_copy(r, 0, p1_ref).start()
        row_copy(r, 1, p2_ref).start()
        return carry
    lax.fori_loop(0, tm, start, 0, unroll=DMA_ISSUE_UNROLL)

    def wait(r, carry):
        row_copy(r, 0, p1_ref).wait()
        row_copy(r, 1, p2_ref).wait()
        return carry
    lax.fori_loop(0, tm, wait, 0, unroll=DMA_ISSUE_UNROLL)
    h = h_ref[...] + (g1_ref[...] * buf[0] + g2_ref[...] * buf[1])
    o_ref[...] = _rms(h, fn_ref[...])


def _combine_norm(h, y, pos1, pos2, g1, g2, final_norm):
    t, d = h.shape
    tm = 256
    col = pl.BlockSpec((tm, 1), lambda i, a, b: (i, 0))
    row = pl.BlockSpec((tm, d), lambda i, a, b: (i, 0))
    grid_spec = pltpu.PrefetchScalarGridSpec(
        num_scalar_prefetch=2,
        grid=(t // tm,),
        in_specs=[row, col, col, pl.BlockSpec((1, d), lambda i, a, b: (0, 0)),
                  pl.BlockSpec(memory_space=pl.ANY)],
        out_specs=row,
        scratch_shapes=[pltpu.VMEM((TOP_K, tm, d), F32), pltpu.SemaphoreType.DMA(())],
    )
    return pl.pallas_call(
        functools.partial(_combine_kernel, tm=tm),
        grid_spec=grid_spec,
        out_shape=jax.ShapeDtypeStruct((t, d), F32),
        compiler_params=_params("arbitrary"),
        name="moe_combine_norm",
    )(pos1, pos2, h, g1, g2, final_norm.reshape(1, d), y)


TM_DENSE = 1024
TN_DENSE = 512
TN_SWIGLU = 256
TM_EXPERT = 1024
TN_SWIGLU_EXPERT = 256
DOWN_K_SPLITS = 4


def _down_proj(a, w, plan, res, tm, tn):
    k = w.shape[1]
    splits = DOWN_K_SPLITS if k > 8192 else 1
    kb = k // splits
    for ki in range(splits):
        res = _matmul(a, (w,), plan, mode="residual", tm=tm, tn=tn, out_dtype=F32,
                      k_blk=kb, k_idx=ki, extras=(res,))
    return res


def kernel(x, positions, mix_norm, ffn_norm, dil_w_in, dil_w_out, moba_w_in, moba_w_out,
           ffn_w_gate, ffn_w_up, ffn_w_down, router_w, exp_w_gate, exp_w_up, exp_w_down,
           final_norm):
    batch, s, d = x.shape
    t = batch * s
    attn_w = dil_w_out.shape[1]
    h = x.reshape(t, d)
    rope = _rope_tables(positions)
    dense = _dense_plan(t // TM_DENSE, TM_DENSE)

    hn = _rmsnorm(h, mix_norm[0])
    proj = _matmul(hn, (dil_w_in,), dense, mode="rope", tm=TM_DENSE, tn=TN_DENSE, out_dtype=BF16,
                   extras=rope, qkv_width=attn_w)
    o = _dilated_attention(proj, batch, attn_w)
    h = _matmul(o, (dil_w_out,), dense, mode="residual", tm=TM_DENSE, tn=TN_DENSE, out_dtype=F32,
                extras=(h,))
    hn = _rmsnorm(h, ffn_norm[0])
    a = _matmul(hn, (ffn_w_gate, ffn_w_up), dense, mode="swiglu", tm=TM_DENSE, tn=TN_SWIGLU,
                out_dtype=BF16)
    h = _down_proj(a, ffn_w_down, dense, h, TM_DENSE, TN_DENSE)

    hn = _rmsnorm(h, mix_norm[1])
    proj = _matmul(hn, (moba_w_in,), dense, mode="rope", tm=TM_DENSE, tn=TN_DENSE, out_dtype=BF16,
                   extras=rope, qkv_width=attn_w)
    o = _moba_attention(proj, batch, attn_w)
    h = _matmul(o, (moba_w_out,), dense, mode="residual", tm=TM_DENSE, tn=TN_DENSE, out_dtype=F32,
                extras=(h,))

    n_experts = router_w.shape[-1]
    i1, i2, g1, g2 = _router(h, ffn_norm[1], router_w[0])
    plan, row_token, g_valid, pos1, pos2 = _route_plan(i1[:, 0], i2[:, 0], n_experts, TM_EXPERT)
    xs = _gather_norm(h, ffn_norm[1], row_token, g_valid, GATHER_ROWS)
    a = _matmul(xs, (exp_w_gate[0], exp_w_up[0]), plan, mode="swiglu", tm=TM_EXPERT,
                tn=TN_SWIGLU_EXPERT, out_dtype=BF16, ragged=True)
    y = _matmul(a, (exp_w_down[0],), plan, mode="plain", tm=TM_EXPERT, tn=TN_DENSE, out_dtype=F32,
                ragged=True)
    out = _combine_norm(h, y, pos1, pos2, g1, g2, final_norm)
    return out.reshape(batch, s, d)
```

```python
import functools

import jax
import jax.numpy as jnp
from jax import lax
from jax.experimental import pallas as pl
from jax.experimental.pallas import tpu as pltpu

HEAD_DIM = 128
ROPE_DIM = HEAD_DIM // 4
ROPE_HALF = ROPE_DIM // 2
ROPE_THETA = 500000.0
SCALE = HEAD_DIM ** -0.5
NORM_EPS = 1e-5
TINY = 1e-30
DILATED_PAIRS = ((128, 1), (512, 4), (2048, 16))
MOBA_BLOCK = 256
MOBA_TOPK = 3
TOP_K = 2

LANES = 128
BF16_ROWS = 16
VMEM_LIMIT_BYTES = 56 * 2 ** 20

NEG_INF = float("-inf")
F32 = jnp.float32
BF16 = jnp.bfloat16

_NT_DIMS = (((1,), (1,)), ((), ()))


def _params(*sem):
    return pltpu.CompilerParams(dimension_semantics=sem, vmem_limit_bytes=VMEM_LIMIT_BYTES)


def _rope_table_kernel(pos_ref, invf_ref, c_ref, s1_ref, s2_ref):
    ang = pos_ref[...] * invf_ref[...]
    lane = lax.broadcasted_iota(jnp.int32, ang.shape, 1)
    c = jnp.cos(ang)
    s = jnp.sin(ang)
    c_ref[...] = jnp.where(lane < ROPE_DIM, c, 1.0)
    s1_ref[...] = jnp.where(lane < ROPE_HALF, -s, 0.0)
    s2_ref[...] = jnp.where(lane < ROPE_HALF, 0.0, jnp.where(lane < ROPE_DIM, s, 0.0))


def _rope_tables(positions):
    t = positions.size
    tm = 1024
    pos = positions.reshape(t, 1).astype(F32)
    inv_freq = jnp.power(ROPE_THETA, -jnp.arange(ROPE_HALF, dtype=F32) / ROPE_HALF)
    invf = jnp.zeros((1, LANES), F32).at[0, :ROPE_DIM].set(jnp.tile(inv_freq, 2))
    tab = jax.ShapeDtypeStruct((t, LANES), F32)
    spec = pl.BlockSpec((tm, LANES), lambda i: (i, 0))
    return pl.pallas_call(
        _rope_table_kernel,
        grid=(t // tm,),
        in_specs=[pl.BlockSpec((tm, 1), lambda i: (i, 0)), pl.BlockSpec((1, LANES), lambda i: (0, 0))],
        out_specs=[spec, spec, spec],
        out_shape=[tab, tab, tab],
        compiler_params=_params("parallel"),
        name="rope_tables",
    )(pos, invf)


def _rms(x, g):
    ms = jnp.mean(x * x, axis=-1, keepdims=True)
    return x * lax.rsqrt(ms + NORM_EPS) * g


def _rmsnorm_kernel(x_ref, g_ref, o_ref):
    o_ref[...] = _rms(x_ref[...], g_ref[...]).astype(o_ref.dtype)


def _rmsnorm(x, g, out_dtype=BF16):
    t, d = x.shape
    tm = 512
    return pl.pallas_call(
        _rmsnorm_kernel,
        grid=(t // tm,),
        in_specs=[pl.BlockSpec((tm, d), lambda i: (i, 0)), pl.BlockSpec((1, d), lambda i: (0, 0))],
        out_specs=pl.BlockSpec((tm, d), lambda i: (i, 0)),
        out_shape=jax.ShapeDtypeStruct((t, d), out_dtype),
        compiler_params=_params("parallel"),
        name="rmsnorm",
    )(x, g.reshape(1, d))


CAST_ROWS = 256
MM_ROW_CHUNK = 256


def _cast_weight(w_ref, wb_ref):
    def body(c, carry):
        r = pl.multiple_of(c * CAST_ROWS, CAST_ROWS)
        wb_ref[pl.ds(r, CAST_ROWS), :] = w_ref[0, pl.ds(r, CAST_ROWS), :].astype(wb_ref.dtype)
        return carry
    lax.fori_loop(0, wb_ref.shape[0] // CAST_ROWS, body, 0)


def _rope_rotate(acc, c, s1, s2):
    return (acc * c + pltpu.roll(acc, LANES - ROPE_HALF, 1) * s1
            + pltpu.roll(acc, ROPE_HALF, 1) * s2)


def _mm_kernel(grp_ref, first_ref, nrows_ref, x_ref, *rest, mode, qkv_width):
    del grp_ref
    j = pl.program_id(0)
    i = pl.program_id(1)
    if mode == "swiglu":
        wg_ref, wu_ref, o_ref, wgb, wub = rest
        weights = ((wg_ref, wgb), (wu_ref, wub))
    elif mode == "residual":
        w_ref, res_ref, o_ref, wb = rest
        weights = ((w_ref, wb),)
    elif mode == "rope":
        w_ref, c_ref, s1_ref, s2_ref, o_ref, wb = rest
        weights = ((w_ref, wb),)
    else:
        w_ref, o_ref, wb = rest
        weights = ((w_ref, wb),)

    @pl.when(first_ref[i] == 1)
    def _cast():
        for w, b in weights:
            _cast_weight(w, b)

    tm, tn = o_ref.shape

    def chunk(r0):
        rows = slice(r0, r0 + MM_ROW_CHUNK)
        x = x_ref[rows, :]
        if mode == "swiglu":
            g = jnp.dot(x, wgb[...], preferred_element_type=F32)
            u = jnp.dot(x, wub[...], preferred_element_type=F32)
            o_ref[rows, :] = (g * (1.0 / (1.0 + jnp.exp(-g))) * u).astype(o_ref.dtype)
            return
        acc = jnp.dot(x, wb[...], preferred_element_type=F32)
        if mode == "residual":
            o_ref[rows, :] = res_ref[rows, :] + acc
        elif mode == "rope":
            is_v = (((j * tn) // qkv_width) % 3 == 2).astype(jnp.int32)
            keep = jnp.broadcast_to(is_v, (MM_ROW_CHUNK, HEAD_DIM)) > 0
            c, s1, s2 = c_ref[rows, :], s1_ref[rows, :], s2_ref[rows, :]
            for hh in range(tn // HEAD_DIM):
                cs = slice(hh * HEAD_DIM, (hh + 1) * HEAD_DIM)
                a = acc[:, cs]
                o_ref[rows, cs] = jnp.where(keep, a, _rope_rotate(a, c, s1, s2)).astype(o_ref.dtype)
        else:
            o_ref[rows, :] = acc.astype(o_ref.dtype)

    @pl.when(nrows_ref[i] == 0)
    def _skip():
        o_ref[...] = jnp.zeros(o_ref.shape, o_ref.dtype)

    @pl.when(nrows_ref[i] > 0)
    def _compute():
        for r0 in range(0, tm, MM_ROW_CHUNK):
            chunk(r0)


def _dense_plan(n_tiles, tm):
    grp = jnp.zeros((n_tiles,), jnp.int32)
    first = jnp.zeros((n_tiles,), jnp.int32).at[0].set(1)
    nrows = jnp.full((n_tiles,), tm, jnp.int32)
    return grp, first, nrows


def _matmul(x, ws, plan, *, mode, tm, tn, out_dtype, k_blk=None, k_idx=0, extras=(), qkv_width=0):
    m = x.shape[0]
    _, k, n = ws[0].shape
    kb = k if k_blk is None else k_blk
    grp, first, valid = plan
    x_spec = pl.BlockSpec((tm, kb), lambda j, i, g, f, v: (i, k_idx))
    w_spec = pl.BlockSpec((1, kb, tn), lambda j, i, g, f, v: (g[i], k_idx, j))
    o_spec = pl.BlockSpec((tm, tn), lambda j, i, g, f, v: (i, j))
    in_specs = [x_spec] + [w_spec] * len(ws)
    if mode == "residual":
        in_specs.append(o_spec)
    elif mode == "rope":
        in_specs += [pl.BlockSpec((tm, LANES), lambda j, i, g, f, v: (i, 0))] * 3
    grid_spec = pltpu.PrefetchScalarGridSpec(
        num_scalar_prefetch=3,
        grid=(n // tn, m // tm),
        in_specs=in_specs,
        out_specs=o_spec,
        scratch_shapes=[pltpu.VMEM((kb, tn), BF16)] * len(ws),
    )
    return pl.pallas_call(
        functools.partial(_mm_kernel, mode=mode, qkv_width=qkv_width),
        grid_spec=grid_spec,
        out_shape=jax.ShapeDtypeStruct((m, n), out_dtype),
        compiler_params=_params("arbitrary", "arbitrary"),
        name="mm_" + mode,
    )(grp, first, valid, x, *ws, *extras)


SPAN = 128
DIL_BLOCK_UNROLL = 16
DIL_VT_UNROLL = 8
MAX_ROW_STRIDE = 8


def _dil_kernel(*refs, seq):
    qkv = refs[:9]
    o_ref = refs[9]
    stage, qp, kp, vt2, og, lg, o_run, l_run = refs[10:]
    n_blocks = seq // SPAN

    key2 = lax.broadcasted_iota(jnp.int32, (2 * SPAN, SPAN), 0)
    qry2 = lax.broadcasted_iota(jnp.int32, (2 * SPAN, SPAN), 1)
    in_cur = jnp.where(key2 >= SPAN, jnp.where(key2 - SPAN <= qry2, 0.0, NEG_INF), NEG_INF)
    bias_first = in_cur
    bias_both = jnp.where(key2 < SPAN, jnp.where(key2 >= qry2, 0.0, NEG_INF), in_cur)

    kp[0:SPAN, :] = jnp.zeros((SPAN, HEAD_DIM), BF16)
    vt2[0, :, 0:SPAN] = jnp.zeros((HEAD_DIM, SPAN), BF16)

    for g, (window, r) in enumerate(DILATED_PAIRS):
        q_ref, k_ref, v_ref = qkv[3 * g:3 * g + 3]
        sub_len = seq // r
        nblk = sub_len // SPAN
        last = g == len(DILATED_PAIRS) - 1

        def regroup(src_ref, dst_ref, off, r=r, sub_len=sub_len):
            if r == 1:
                dst_ref[off:off + seq, :] = src_ref[0].astype(dst_ref.dtype)
                return
            stage[...] = src_ref[0].astype(F32)
            if r <= MAX_ROW_STRIDE:
                for rr in range(r):
                    dst_ref[off + rr * sub_len:off + (rr + 1) * sub_len, :] = (
                        stage[pl.ds(rr, sub_len, stride=r), :].astype(dst_ref.dtype))
                return
            r1 = MAX_ROW_STRIDE // 2
            r2 = r // r1
            assert r % r1 == 0 and r2 <= MAX_ROW_STRIDE
            len1 = seq // r1
            for a in range(r1):
                lg[a * len1:(a + 1) * len1, :] = stage[pl.ds(a, len1, stride=r1), :]
            for rr in range(r):
                a, k = rr % r1, rr // r1
                dst_ref[off + rr * sub_len:off + (rr + 1) * sub_len, :] = (
                    lg[pl.ds(a * len1 + k, sub_len, stride=r2), :].astype(dst_ref.dtype))

        regroup(q_ref, qp, 0)
        regroup(k_ref, kp, SPAN)
        regroup(v_ref, og, 0)

        def vt_body(i, carry):
            base = pl.multiple_of(i * SPAN, SPAN)
            tr = og[pl.ds(base, SPAN), :].T.astype(BF16)
            vt2[i, :, SPAN:2 * SPAN] = tr
            vt2[i + 1, :, 0:SPAN] = tr
            return carry
        lax.fori_loop(0, n_blocks, vt_body, 0, unroll=DIL_VT_UNROLL)

        out_o, out_l = (o_run, l_run) if g == 0 else (og, lg)

        def blk_body(i, carry, nblk=nblk, out_o=out_o, out_l=out_l):
            base = pl.multiple_of(i * SPAN, SPAN)
            q = qp[pl.ds(base, SPAN), :]
            has_prev = jnp.broadcast_to((lax.rem(i, nblk) > 0).astype(jnp.int32), (2 * SPAN, SPAN)) > 0
            s = lax.dot_general(kp[pl.ds(base, 2 * SPAN), :], q, _NT_DIMS,
                                preferred_element_type=F32) * SCALE + jnp.where(has_prev, bias_both, bias_first)
            m = jnp.max(s, axis=0, keepdims=True)
            p = jnp.exp(s - m)
            den = jnp.sum(p, axis=0, keepdims=True)
            o_t = jnp.dot(vt2[i], p.astype(BF16), preferred_element_type=F32)
            o_t = o_t / jnp.maximum(den, TINY)
            lse = m + jnp.log(den)
            out_o[pl.ds(base, SPAN), :] = o_t.T
            out_l[pl.ds(base, SPAN), :] = jnp.broadcast_to(lse, (SPAN, SPAN)).T
            return carry
        lax.fori_loop(0, n_blocks, blk_body, 0, unroll=DIL_BLOCK_UNROLL)

        if g > 0:
            for rr in range(r):
                for c in range(sub_len // SPAN):
                    tok = pl.ds(rr + c * SPAN * r, SPAN, stride=r)
                    rows = pl.ds(rr * sub_len + c * SPAN, SPAN)
                    lp = l_run[tok, :]
                    ln = lg[rows, :]
                    mx = jnp.maximum(lp, ln)
                    wp = jnp.exp(lp - mx)
                    wn = jnp.exp(ln - mx)
                    tot = wp + wn
                    o_run[tok, :] = (o_run[tok, :] * wp + og[rows, :] * wn) / tot
                    if not last:
                        l_run[tok, :] = mx + jnp.log(tot)

    o_ref[0] = o_run[...].astype(o_ref.dtype)


def _dilated_attention(proj, batch, attn_w):
    t, e = proj.shape
    s = t // batch
    n_heads = attn_w // HEAD_DIM
    for window, r in DILATED_PAIRS:
        assert window // r == SPAN and s % (r * SPAN) == 0
    pv = proj.reshape(batch, s, e)

    def spec(g, c):
        return pl.BlockSpec((1, s, HEAD_DIM), lambda b, h: (b, 0, (g * 3 + c) * n_heads + h))

    seq_f32 = pltpu.VMEM((s, HEAD_DIM), F32)
    seq_bf16 = pltpu.VMEM((s, HEAD_DIM), BF16)
    out = pl.pallas_call(
        functools.partial(_dil_kernel, seq=s),
        grid=(batch, n_heads),
        in_specs=[spec(g, c) for g in range(len(DILATED_PAIRS)) for c in range(3)],
        out_specs=pl.BlockSpec((1, s, HEAD_DIM), lambda b, h: (b, 0, h)),
        out_shape=jax.ShapeDtypeStruct((batch, s, attn_w), BF16),
        scratch_shapes=[seq_f32, seq_bf16, pltpu.VMEM((s + SPAN, HEAD_DIM), BF16),
                        pltpu.VMEM((s // SPAN + 1, HEAD_DIM, 2 * SPAN), BF16),
                        seq_f32, seq_f32, seq_f32, seq_f32],
        compiler_params=_params("parallel", "parallel"),
        name="dilated_attn",
    )(*([pv] * 9))
    return out.reshape(t, attn_w)


MOBA_HEADS_PER_STEP = 2
MOBA_BLOCKS_PER_ITER = 4


def _moba_kernel(q_ref, k_ref, v_ref, o_ref, vt, kmean, kparts, sel_ref, acc_ref, *, seq):
    blk = MOBA_BLOCK
    nb = seq // blk
    nbp = kmean.shape[1]
    heads = range(MOBA_HEADS_PER_STEP)
    per_it = MOBA_BLOCKS_PER_ITER
    kmean[...] = jnp.zeros(kmean.shape, F32)
    sel_ref[...] = jnp.full(sel_ref.shape, NEG_INF, F32)

    def head_cols(h):
        return slice(h * HEAD_DIM, (h + 1) * HEAD_DIM)

    def pre(jb, carry):
        r0 = pl.multiple_of(jb * blk, blk)
        for h in heads:
            vt[h, jb] = v_ref[0, pl.ds(r0, blk), head_cols(h)].astype(F32).T.astype(BF16)
            kb = k_ref[0, pl.ds(r0, blk), head_cols(h)].astype(F32)
            kmean[h, pl.ds(jb, 1), :] = jnp.sum(kb, axis=0, keepdims=True) * (1.0 / blk)
        return carry
    lax.fori_loop(0, nb, pre, 0)

    for h in heads:
        km = kmean[h]
        hi = km.astype(BF16)
        r1 = km - hi.astype(F32)
        mid = r1.astype(BF16)
        lo = (r1 - mid.astype(F32)).astype(BF16)
        kparts[h, 0:nbp, :] = hi
        kparts[h, nbp:2 * nbp, :] = mid
        kparts[h, 2 * nbp:3 * nbp, :] = lo

    kblk = lax.broadcasted_iota(jnp.int32, (nbp, blk), 0)
    key = lax.broadcasted_iota(jnp.int32, (blk, blk), 0)
    qry = lax.broadcasted_iota(jnp.int32, (blk, blk), 1)
    bias_causal = jnp.where(key <= qry, 0.0, NEG_INF)

    def q_body(qi, carry):
        r0 = pl.multiple_of(qi * blk, blk)
        qs, ms, ls = [], [], []
        for h in heads:
            q = q_ref[0, pl.ds(r0, blk), head_cols(h)]
            g3 = lax.dot_general(kparts[h], q, _NT_DIMS, preferred_element_type=F32)
            gate = g3[0:nbp] + g3[nbp:2 * nbp] + g3[2 * nbp:3 * nbp]
            gate = jnp.where(kblk < qi, gate, NEG_INF)
            beaten = jnp.zeros((nbp, blk), F32)
            for jp in range(nb):
                gj = gate[jp:jp + 1, :]
                tie = jnp.where(kblk > jp, 1.0, 0.0)
                beaten = beaten + jnp.where(gj > gate, 1.0, jnp.where(gj == gate, tie, 0.0))
            sel_ref[h, 0:nbp, :] = jnp.where(beaten < MOBA_TOPK, jnp.where(kblk < qi, 0.0, NEG_INF), NEG_INF)

            s = lax.dot_general(k_ref[0, pl.ds(r0, blk), head_cols(h)], q, _NT_DIMS,
                                preferred_element_type=F32) * SCALE + bias_causal
            m = jnp.max(s, axis=0, keepdims=True)
            p = jnp.exp(s - m)
            acc_ref[h] = jnp.dot(vt[h, qi], p.astype(BF16), preferred_element_type=F32)
            qs.append(q)
            ms.append(m)
            ls.append(jnp.sum(p, axis=0, keepdims=True))

        def past(it, st):
            ms, ls = st
            new_m, new_l = [], []
            for h in heads:
                m, l = ms[h], ls[h]
                blocks = [it * per_it + u for u in range(per_it)]
                ss = []
                for jb in blocks:
                    c0 = pl.multiple_of(jnp.minimum(jb, nb - 1) * blk, blk)
                    ss.append(lax.dot_general(k_ref[0, pl.ds(c0, blk), head_cols(h)], qs[h], _NT_DIMS,
                                              preferred_element_type=F32) * SCALE + sel_ref[h, pl.ds(jb, 1), :])
                m_new = m
                for s in ss:
                    m_new = jnp.maximum(m_new, jnp.max(s, axis=0, keepdims=True))
                alpha = jnp.exp(m - m_new)
                l = alpha * l
                acc = alpha * acc_ref[h]
                for jb, s in zip(blocks, ss):
                    p = jnp.exp(s - m_new)
                    l = l + jnp.sum(p, axis=0, keepdims=True)
                    acc = acc + jnp.dot(vt[h, jnp.minimum(jb, nb - 1)], p.astype(BF16),
                                        preferred_element_type=F32)
                acc_ref[h] = acc
                new_m.append(m_new)
                new_l.append(l)
            return tuple(new_m), tuple(new_l)

        ms, ls = lax.fori_loop(0, (qi + per_it - 1) // per_it, past, (tuple(ms), tuple(ls)))
        for h in heads:
            o_ref[0, pl.ds(r0, blk), head_cols(h)] = (
                (acc_ref[h] / jnp.maximum(ls[h], TINY)).T.astype(o_ref.dtype))
        return carry

    lax.fori_loop(0, nb, q_body, 0)


def _moba_attention(proj, batch, attn_w):
    t, e = proj.shape
    s = t // batch
    assert s % MOBA_BLOCK == 0
    nb = s // MOBA_BLOCK
    nbp = -(-nb // BF16_ROWS) * BF16_ROWS
    n_heads = attn_w // HEAD_DIM
    hp = MOBA_HEADS_PER_STEP
    assert n_heads % hp == 0
    w = hp * HEAD_DIM
    pv = proj.reshape(batch, s, e)

    def spec(c):
        return pl.BlockSpec((1, s, w), lambda b, h: (b, 0, c * (n_heads // hp) + h))

    out = pl.pallas_call(
        functools.partial(_moba_kernel, seq=s),
        grid=(batch, n_heads // hp),
        in_specs=[spec(0), spec(1), spec(2)],
        out_specs=pl.BlockSpec((1, s, w), lambda b, h: (b, 0, h)),
        out_shape=jax.ShapeDtypeStruct((batch, s, attn_w), BF16),
        scratch_shapes=[pltpu.VMEM((hp, nb, HEAD_DIM, MOBA_BLOCK), BF16),
                        pltpu.VMEM((hp, nbp, HEAD_DIM), F32),
                        pltpu.VMEM((hp, 3 * nbp, HEAD_DIM), BF16),
                        pltpu.VMEM((hp, nbp + MOBA_BLOCKS_PER_ITER, MOBA_BLOCK), F32),
                        pltpu.VMEM((hp, HEAD_DIM, MOBA_BLOCK), F32)],
        compiler_params=_params("parallel", "parallel"),
        name="moba_attn",
    )(pv, pv, pv)
    return out.reshape(t, attn_w)


def _router_kernel(x_ref, g_ref, wr_ref, i1_ref, i2_ref, g1_ref, g2_ref, *, n_experts):
    y = _rms(x_ref[...], g_ref[...])
    lg = jnp.dot(y, wr_ref[...], preferred_element_type=F32, precision=lax.Precision.HIGHEST)
    lane = lax.broadcasted_iota(jnp.int32, lg.shape, 1).astype(F32)
    lg = jnp.where(lane < n_experts, lg, NEG_INF)
    m1 = jnp.max(lg, axis=1, keepdims=True)
    i1 = jnp.min(jnp.where(lg == m1, lane, float(LANES)), axis=1, keepdims=True)
    lg2 = jnp.where(lane == i1, NEG_INF, lg)
    m2 = jnp.max(lg2, axis=1, keepdims=True)
    i2 = jnp.min(jnp.where(lg2 == m2, lane, float(LANES)), axis=1, keepdims=True)
    e = jnp.exp(m2 - m1)
    i1_ref[...] = i1.astype(jnp.int32)
    i2_ref[...] = i2.astype(jnp.int32)
    g1_ref[...] = 1.0 / (1.0 + e)
    g2_ref[...] = e / (1.0 + e)


def _router(h, g, w_router):
    t, d = h.shape
    n_experts = w_router.shape[1]
    assert n_experts <= LANES
    tm = 256
    wr = jnp.zeros((d, LANES), F32).at[:, :n_experts].set(w_router)
    col = pl.BlockSpec((tm, 1), lambda i: (i, 0))
    return pl.pallas_call(
        functools.partial(_router_kernel, n_experts=n_experts),
        grid=(t // tm,),
        in_specs=[pl.BlockSpec((tm, d), lambda i: (i, 0)), pl.BlockSpec((1, d), lambda i: (0, 0)),
                  pl.BlockSpec((d, LANES), lambda i: (0, 0))],
        out_specs=[col, col, col, col],
        out_shape=[jax.ShapeDtypeStruct((t, 1), jnp.int32)] * 2 + [jax.ShapeDtypeStruct((t, 1), F32)] * 2,
        compiler_params=_params("parallel"),
        name="router",
    )(h, g.reshape(1, d), wr)


def _route_plan(i1, i2, n_experts, tm):
    t = i1.shape[0]
    e = jnp.concatenate([i1, i2])
    onehot = (e[:, None] == jnp.arange(n_experts, dtype=jnp.int32)[None, :]).astype(jnp.int32)
    csum = jnp.cumsum(onehot, axis=0)
    rank = jnp.take_along_axis(csum, e[:, None], axis=1)[:, 0] - 1
    counts = csum[-1]
    padded = ((counts + tm - 1) // tm) * tm
    ends = jnp.cumsum(padded)
    starts = ends - padded
    pos = starts[e] + rank
    n_rows = TOP_K * t + n_experts * tm
    tok = jnp.arange(t, dtype=jnp.int32)
    row_token = jnp.zeros((n_rows,), jnp.int32).at[pos].set(jnp.concatenate([tok, tok]))
    tile_start = jnp.arange(n_rows // tm, dtype=jnp.int32) * tm
    grp = jnp.sum((tile_start[:, None] >= ends[None, :]).astype(jnp.int32), axis=1)
    in_use = grp < n_experts
    grp = jnp.minimum(grp, n_experts - 1)
    nrows = jnp.where(in_use, jnp.clip((starts + counts)[grp] - tile_start, 0, tm), 0).astype(jnp.int32)
    first = jnp.concatenate([jnp.ones((1,), jnp.int32), (grp[1:] != grp[:-1]).astype(jnp.int32)])
    g_start = jnp.arange(n_rows // GATHER_ROWS, dtype=jnp.int32) * GATHER_ROWS
    g_valid = ((g_start % tm) < nrows[g_start // tm]).astype(jnp.int32)
    return (grp, first, nrows), row_token, g_valid, pos[:t], pos[t:]


GATHER_ROWS = 256
DMA_ISSUE_UNROLL = 8


def _gather_norm_kernel(tok_ref, valid_ref, g_ref, h_hbm, o_ref, buf, sem, *, tm):
    i = pl.program_id(0)

    def row_copy(r):
        return pltpu.make_async_copy(h_hbm.at[pl.ds(tok_ref[i * tm + r], 1)], buf.at[pl.ds(r, 1)], sem)

    @pl.when(valid_ref[i] == 0)
    def _skip():
        o_ref[...] = jnp.zeros(o_ref.shape, o_ref.dtype)

    @pl.when(valid_ref[i] == 1)
    def _gather():
        def start(r, carry):
            row_copy(r).start()
            return carry
        lax.fori_loop(0, tm, start, 0, unroll=DMA_ISSUE_UNROLL)

        def wait(r, carry):
            row_copy(r).wait()
            return carry
        lax.fori_loop(0, tm, wait, 0, unroll=DMA_ISSUE_UNROLL)
        o_ref[...] = _rms(buf[...], g_ref[...]).astype(o_ref.dtype)


def _gather_norm(h, g, row_token, valid, tm):
    t, d = h.shape
    n_rows = row_token.shape[0]
    grid_spec = pltpu.PrefetchScalarGridSpec(
        num_scalar_prefetch=2,
        grid=(n_rows // tm,),
        in_specs=[pl.BlockSpec((1, d), lambda i, tok, v: (0, 0)), pl.BlockSpec(memory_space=pl.ANY)],
        out_specs=pl.BlockSpec((tm, d), lambda i, tok, v: (i, 0)),
        scratch_shapes=[pltpu.VMEM((tm, d), F32), pltpu.SemaphoreType.DMA(())],
    )
    return pl.pallas_call(
        functools.partial(_gather_norm_kernel, tm=tm),
        grid_spec=grid_spec,
        out_shape=jax.ShapeDtypeStruct((n_rows, d), BF16),
        compiler_params=_params("arbitrary"),
        name="moe_gather_norm",
    )(row_token, valid, g.reshape(1, d), h)


def _combine_kernel(p1_ref, p2_ref, h_ref, g1_ref, g2_ref, fn_ref, y_hbm, o_ref, buf, sem, *, tm):
    i = pl.program_id(0)

    def row_copy(r, which, p_ref):
        return pltpu.make_async_copy(y_hbm.at[pl.ds(p_ref[i * tm + r], 1)], buf.at[which, pl.ds(r, 1)], sem)

    def start(r, carry):
        row_copy(r, 0, p1_ref).start()
        row_copy(r, 1, p2_ref).start()
        return carry
    lax.fori_loop(0, tm, start, 0, unroll=DMA_ISSUE_UNROLL)

    def wait(r, carry):
        row_copy(r, 0, p1_ref).wait()
        row_copy(r, 1, p2_ref).wait()
        return carry
    lax.fori_loop(0, tm, wait, 0, unroll=DMA_ISSUE_UNROLL)
    h = h_ref[...] + (g1_ref[...] * buf[0] + g2_ref[...] * buf[1])
    o_ref[...] = _rms(h, fn_ref[...])


def _combine_norm(h, y, pos1, pos2, g1, g2, final_norm):
    t, d = h.shape
    tm = 256
    col = pl.BlockSpec((tm, 1), lambda i, a, b: (i, 0))
    row = pl.BlockSpec((tm, d), lambda i, a, b: (i, 0))
    grid_spec = pltpu.PrefetchScalarGridSpec(
        num_scalar_prefetch=2,
        grid=(t // tm,),
        in_specs=[row, col, col, pl.BlockSpec((1, d), lambda i, a, b: (0, 0)),
                  pl.BlockSpec(memory_space=pl.ANY)],
        out_specs=row,
        scratch_shapes=[pltpu.VMEM((TOP_K, tm, d), F32), pltpu.SemaphoreType.DMA(())],
    )
    return pl.pallas_call(
        functools.partial(_combine_kernel, tm=tm),
        grid_spec=grid_spec,
        out_shape=jax.ShapeDtypeStruct((t, d), F32),
        compiler_params=_params("arbitrary"),
        name="moe_combine_norm",
    )(pos1, pos2, h, g1, g2, final_norm.reshape(1, d), y)


TM_DENSE = 1024
TN_DENSE = 512
TN_SWIGLU = 256
TM_EXPERT = 512
TN_SWIGLU_EXPERT = 512
TN_DOWN_EXPERT = 1024
DOWN_K_SPLITS = 4


def _down_proj(a, w, plan, res, tm, tn):
    k = w.shape[1]
    splits = DOWN_K_SPLITS if k > 8192 else 1
    kb = k // splits
    for ki in range(splits):
        res = _matmul(a, (w,), plan, mode="residual", tm=tm, tn=tn, out_dtype=F32,
                      k_blk=kb, k_idx=ki, extras=(res,))
    return res


def kernel(x, positions, mix_norm, ffn_norm, dil_w_in, dil_w_out, moba_w_in, moba_w_out,
           ffn_w_gate, ffn_w_up, ffn_w_down, router_w, exp_w_gate, exp_w_up, exp_w_down,
           final_norm):
    batch, s, d = x.shape
    t = batch * s
    attn_w = dil_w_out.shape[1]
    h = x.reshape(t, d)
    rope = _rope_tables(positions)
    dense = _dense_plan(t // TM_DENSE, TM_DENSE)

    hn = _rmsnorm(h, mix_norm[0])
    proj = _matmul(hn, (dil_w_in,), dense, mode="rope", tm=TM_DENSE, tn=TN_DENSE, out_dtype=BF16,
                   extras=rope, qkv_width=attn_w)
    o = _dilated_attention(proj, batch, attn_w)
    h = _matmul(o, (dil_w_out,), dense, mode="residual", tm=TM_DENSE, tn=TN_DENSE, out_dtype=F32,
                extras=(h,))
    hn = _rmsnorm(h, ffn_norm[0])
    a = _matmul(hn, (ffn_w_gate, ffn_w_up), dense, mode="swiglu", tm=TM_DENSE, tn=TN_SWIGLU,
                out_dtype=BF16)
    h = _down_proj(a, ffn_w_down, dense, h, TM_DENSE, TN_DENSE)

    hn = _rmsnorm(h, mix_norm[1])
    proj = _matmul(hn, (moba_w_in,), dense, mode="rope", tm=TM_DENSE, tn=TN_DENSE, out_dtype=BF16,
                   extras=rope, qkv_width=attn_w)
    o = _moba_attention(proj, batch, attn_w)
    h = _matmul(o, (moba_w_out,), dense, mode="residual", tm=TM_DENSE, tn=TN_DENSE, out_dtype=F32,
                extras=(h,))

    n_experts = router_w.shape[-1]
    i1, i2, g1, g2 = _router(h, ffn_norm[1], router_w[0])
    plan, row_token, g_valid, pos1, pos2 = _route_plan(i1[:, 0], i2[:, 0], n_experts, TM_EXPERT)
    xs = _gather_norm(h, ffn_norm[1], row_token, g_valid, GATHER_ROWS)
    a = _matmul(xs, (exp_w_gate[0], exp_w_up[0]), plan, mode="swiglu", tm=TM_EXPERT,
                tn=TN_SWIGLU_EXPERT, out_dtype=BF16)
    y = _matmul(a, (exp_w_down[0],), plan, mode="plain", tm=TM_EXPERT, tn=TN_DOWN_EXPERT, out_dtype=F32)
    out = _combine_norm(h, y, pos1, pos2, g1, g2, final_norm)
    return out.reshape(batch, s, d)
```

```python
import functools

import jax
import jax.numpy as jnp
from jax import lax
from jax.experimental import pallas as pl
from jax.experimental.pallas import tpu as pltpu

HEAD_DIM = 128
ROPE_DIM = HEAD_DIM // 4
ROPE_HALF = ROPE_DIM // 2
ROPE_THETA = 500000.0
SCALE = HEAD_DIM ** -0.5
NORM_EPS = 1e-5
TINY = 1e-30
DILATED_PAIRS = ((128, 1), (512, 4), (2048, 16))
MOBA_BLOCK = 256
MOBA_TOPK = 3
TOP_K = 2

LANES = 128
BF16_ROWS = 16
VMEM_LIMIT_BYTES = 56 * 2 ** 20

NEG_INF = float("-inf")
F32 = jnp.float32
BF16 = jnp.bfloat16

_NT_DIMS = (((1,), (1,)), ((), ()))


def _params(*sem):
    return pltpu.CompilerParams(dimension_semantics=sem, vmem_limit_bytes=VMEM_LIMIT_BYTES)


def _rope_table_kernel(pos_ref, invf_ref, c_ref, s1_ref, s2_ref):
    ang = pos_ref[...] * invf_ref[...]
    lane = lax.broadcasted_iota(jnp.int32, ang.shape, 1)
    c = jnp.cos(ang)
    s = jnp.sin(ang)
    c_ref[...] = jnp.where(lane < ROPE_DIM, c, 1.0)
    s1_ref[...] = jnp.where(lane < ROPE_HALF, -s, 0.0)
    s2_ref[...] = jnp.where(lane < ROPE_HALF, 0.0, jnp.where(lane < ROPE_DIM, s, 0.0))


def _rope_tables(positions):
    t = positions.size
    tm = 1024
    pos = positions.reshape(t, 1).astype(F32)
    inv_freq = jnp.power(ROPE_THETA, -jnp.arange(ROPE_HALF, dtype=F32) / ROPE_HALF)
    invf = jnp.zeros((1, LANES), F32).at[0, :ROPE_DIM].set(jnp.tile(inv_freq, 2))
    tab = jax.ShapeDtypeStruct((t, LANES), F32)
    spec = pl.BlockSpec((tm, LANES), lambda i: (i, 0))
    return pl.pallas_call(
        _rope_table_kernel,
        grid=(t // tm,),
        in_specs=[pl.BlockSpec((tm, 1), lambda i: (i, 0)), pl.BlockSpec((1, LANES), lambda i: (0, 0))],
        out_specs=[spec, spec, spec],
        out_shape=[tab, tab, tab],
        compiler_params=_params("parallel"),
        name="rope_tables",
    )(pos, invf)


def _rms(x, g):
    ms = jnp.mean(x * x, axis=-1, keepdims=True)
    return x * lax.rsqrt(ms + NORM_EPS) * g


def _rmsnorm_kernel(x_ref, g_ref, o_ref):
    o_ref[...] = _rms(x_ref[...], g_ref[...]).astype(o_ref.dtype)


def _rmsnorm(x, g, out_dtype=BF16):
    t, d = x.shape
    tm = 512
    return pl.pallas_call(
        _rmsnorm_kernel,
        grid=(t // tm,),
        in_specs=[pl.BlockSpec((tm, d), lambda i: (i, 0)), pl.BlockSpec((1, d), lambda i: (0, 0))],
        out_specs=pl.BlockSpec((tm, d), lambda i: (i, 0)),
        out_shape=jax.ShapeDtypeStruct((t, d), out_dtype),
        compiler_params=_params("parallel"),
        name="rmsnorm",
    )(x, g.reshape(1, d))


CAST_ROWS = 256
MM_ROW_CHUNK = 256


def _cast_weight(w_ref, wb_ref):
    def body(c, carry):
        r = pl.multiple_of(c * CAST_ROWS, CAST_ROWS)
        wb_ref[pl.ds(r, CAST_ROWS), :] = w_ref[0, pl.ds(r, CAST_ROWS), :].astype(wb_ref.dtype)
        return carry
    lax.fori_loop(0, wb_ref.shape[0] // CAST_ROWS, body, 0)


def _rope_rotate(acc, c, s1, s2):
    return (acc * c + pltpu.roll(acc, LANES - ROPE_HALF, 1) * s1
            + pltpu.roll(acc, ROPE_HALF, 1) * s2)


def _mm_kernel(grp_ref, first_ref, nrows_ref, x_ref, *rest, mode, qkv_width):
    del grp_ref
    j = pl.program_id(0)
    i = pl.program_id(1)
    if mode == "swiglu":
        wg_ref, wu_ref, o_ref, wgb, wub = rest
        weights = ((wg_ref, wgb), (wu_ref, wub))
    elif mode == "residual":
        w_ref, res_ref, o_ref, wb = rest
        weights = ((w_ref, wb),)
    elif mode == "rope":
        w_ref, c_ref, s1_ref, s2_ref, o_ref, wb = rest
        weights = ((w_ref, wb),)
    else:
        w_ref, o_ref, wb = rest
        weights = ((w_ref, wb),)

    @pl.when(first_ref[i] == 1)
    def _cast():
        for w, b in weights:
            _cast_weight(w, b)

    tm, tn = o_ref.shape

    def chunk(r0):
        rows = slice(r0, r0 + MM_ROW_CHUNK)
        x = x_ref[rows, :]
        if mode == "swiglu":
            g = jnp.dot(x, wgb[...], preferred_element_type=F32)
            u = jnp.dot(x, wub[...], preferred_element_type=F32)
            o_ref[rows, :] = (g * (1.0 / (1.0 + jnp.exp(-g))) * u).astype(o_ref.dtype)
            return
        acc = jnp.dot(x, wb[...], preferred_element_type=F32)
        if mode == "residual":
            o_ref[rows, :] = res_ref[rows, :] + acc
        elif mode == "rope":
            is_v = (((j * tn) // qkv_width) % 3 == 2).astype(jnp.int32)
            keep = jnp.broadcast_to(is_v, (MM_ROW_CHUNK, HEAD_DIM)) > 0
            c, s1, s2 = c_ref[rows, :], s1_ref[rows, :], s2_ref[rows, :]
            for hh in range(tn // HEAD_DIM):
                cs = slice(hh * HEAD_DIM, (hh + 1) * HEAD_DIM)
                a = acc[:, cs]
                o_ref[rows, cs] = jnp.where(keep, a, _rope_rotate(a, c, s1, s2)).astype(o_ref.dtype)
        else:
            o_ref[rows, :] = acc.astype(o_ref.dtype)

    @pl.when(nrows_ref[i] == 0)
    def _skip():
        o_ref[...] = jnp.zeros(o_ref.shape, o_ref.dtype)

    @pl.when(nrows_ref[i] > 0)
    def _compute():
        for r0 in range(0, tm, MM_ROW_CHUNK):
            chunk(r0)


def _dense_plan(n_tiles, tm):
    grp = jnp.zeros((n_tiles,), jnp.int32)
    first = jnp.zeros((n_tiles,), jnp.int32).at[0].set(1)
    nrows = jnp.full((n_tiles,), tm, jnp.int32)
    return grp, first, nrows


def _matmul(x, ws, plan, *, mode, tm, tn, out_dtype, k_blk=None, k_idx=0, extras=(), qkv_width=0):
    m = x.shape[0]
    _, k, n = ws[0].shape
    kb = k if k_blk is None else k_blk
    grp, first, valid = plan
    x_spec = pl.BlockSpec((tm, kb), lambda j, i, g, f, v: (i, k_idx))
    w_spec = pl.BlockSpec((1, kb, tn), lambda j, i, g, f, v: (g[i], k_idx, j))
    o_spec = pl.BlockSpec((tm, tn), lambda j, i, g, f, v: (i, j))
    in_specs = [x_spec] + [w_spec] * len(ws)
    if mode == "residual":
        in_specs.append(o_spec)
    elif mode == "rope":
        in_specs += [pl.BlockSpec((tm, LANES), lambda j, i, g, f, v: (i, 0))] * 3
    grid_spec = pltpu.PrefetchScalarGridSpec(
        num_scalar_prefetch=3,
        grid=(n // tn, m // tm),
        in_specs=in_specs,
        out_specs=o_spec,
        scratch_shapes=[pltpu.VMEM((kb, tn), BF16)] * len(ws),
    )
    return pl.pallas_call(
        functools.partial(_mm_kernel, mode=mode, qkv_width=qkv_width),
        grid_spec=grid_spec,
        out_shape=jax.ShapeDtypeStruct((m, n), out_dtype),
        compiler_params=_params("arbitrary", "arbitrary"),
        name="mm_" + mode,
    )(grp, first, valid, x, *ws, *extras)


STREAM_CAST_ROWS = 128


def _mm_stream_kernel(x_ref, *rest, mode, qkv_width):
    j = pl.program_id(0)
    i = pl.program_id(1)
    if mode == "swiglu":
        wg_ref, wu_ref, o_ref, wgb, wub = rest
        weights = ((wg_ref, wgb), (wu_ref, wub))
    else:
        w_ref, c_ref, s1_ref, s2_ref, o_ref, wb = rest
        weights = ((w_ref, wb),)
    tm, tn = o_ref.shape
    k_chunk = weights[0][0].shape[1]
    nxt = lax.rem(j, 2)
    cur = 1 - nxt

    def cast_chunk():
        base = i * k_chunk
        for w, b in weights:
            for r0 in range(0, k_chunk, STREAM_CAST_ROWS):
                r = pl.multiple_of(base + r0, STREAM_CAST_ROWS)
                b[nxt, pl.ds(r, STREAM_CAST_ROWS), :] = w[0, r0:r0 + STREAM_CAST_ROWS, :].astype(BF16)

    def compute():
        for r0 in range(0, tm, MM_ROW_CHUNK):
            rows = slice(r0, r0 + MM_ROW_CHUNK)
            x = x_ref[rows, :]
            if mode == "swiglu":
                g = jnp.dot(x, wgb[cur], preferred_element_type=F32)
                u = jnp.dot(x, wub[cur], preferred_element_type=F32)
                o_ref[rows, :] = (g * (1.0 / (1.0 + jnp.exp(-g))) * u).astype(o_ref.dtype)
                continue
            acc = jnp.dot(x, wb[cur], preferred_element_type=F32)
            is_v = ((((j - 1) * tn) // qkv_width) % 3 == 2).astype(jnp.int32)
            keep = jnp.broadcast_to(is_v, (MM_ROW_CHUNK, HEAD_DIM)) > 0
            c, s1, s2 = c_ref[rows, :], s1_ref[rows, :], s2_ref[rows, :]
            for hh in range(tn // HEAD_DIM):
                cs = slice(hh * HEAD_DIM, (hh + 1) * HEAD_DIM)
                a = acc[:, cs]
                o_ref[rows, cs] = jnp.where(keep, a, _rope_rotate(a, c, s1, s2)).astype(o_ref.dtype)

    @pl.when(j == 0)
    def _load_only():
        cast_chunk()
        o_ref[...] = jnp.zeros(o_ref.shape, o_ref.dtype)

    @pl.when(j > 0)
    def _load_and_multiply():
        cast_chunk()
        compute()


def _matmul_stream(x, ws, *, mode, tm, tn, out_dtype, extras=(), qkv_width=0):
    m, k = x.shape
    n = ws[0].shape[2]
    m_tiles = m // tm
    n_tiles = n // tn
    k_chunk = k // m_tiles
    assert k_chunk * m_tiles == k and k_chunk % STREAM_CAST_ROWS == 0
    x_spec = pl.BlockSpec((tm, k), lambda j, i: (i, 0))
    w_spec = pl.BlockSpec((1, k_chunk, tn), lambda j, i: (0, i, jnp.minimum(j, n_tiles - 1)))
    o_spec = pl.BlockSpec((tm, tn), lambda j, i: (i, jnp.where(j == 0, n_tiles, j - 1)))
    in_specs = [x_spec] + [w_spec] * len(ws)
    if mode == "rope":
        in_specs += [pl.BlockSpec((tm, LANES), lambda j, i: (i, 0))] * 3
    return pl.pallas_call(
        functools.partial(_mm_stream_kernel, mode=mode, qkv_width=qkv_width),
        grid=(n_tiles + 1, m_tiles),
        in_specs=in_specs,
        out_specs=o_spec,
        out_shape=jax.ShapeDtypeStruct((m, n + tn), out_dtype),
        scratch_shapes=[pltpu.VMEM((2, k, tn), BF16)] * len(ws),
        compiler_params=_params("arbitrary", "arbitrary"),
        name="mms_" + mode,
    )(x, *ws, *extras)


SPAN = 128
DIL_BLOCK_UNROLL = 16
DIL_VT_UNROLL = 8
MAX_ROW_STRIDE = 8


def _dil_kernel(*refs, seq):
    qkv = refs[:9]
    o_ref = refs[9]
    stage, qp, kp, vt2, og, lg, o_run, l_run = refs[10:]
    n_blocks = seq // SPAN

    key2 = lax.broadcasted_iota(jnp.int32, (2 * SPAN, SPAN), 0)
    qry2 = lax.broadcasted_iota(jnp.int32, (2 * SPAN, SPAN), 1)
    in_cur = jnp.where(key2 >= SPAN, jnp.where(key2 - SPAN <= qry2, 0.0, NEG_INF), NEG_INF)
    bias_first = in_cur
    bias_both = jnp.where(key2 < SPAN, jnp.where(key2 >= qry2, 0.0, NEG_INF), in_cur)

    kp[0:SPAN, :] = jnp.zeros((SPAN, HEAD_DIM), BF16)
    vt2[0, :, 0:SPAN] = jnp.zeros((HEAD_DIM, SPAN), BF16)

    for g, (window, r) in enumerate(DILATED_PAIRS):
        q_ref, k_ref, v_ref = qkv[3 * g:3 * g + 3]
        sub_len = seq // r
        nblk = sub_len // SPAN
        last = g == len(DILATED_PAIRS) - 1

        def regroup(src_ref, dst_ref, off, r=r, sub_len=sub_len):
            if r == 1:
                dst_ref[off:off + seq, :] = src_ref[0].astype(dst_ref.dtype)
                return
            stage[...] = src_ref[0].astype(F32)
            if r <= MAX_ROW_STRIDE:
                for rr in range(r):
                    dst_ref[off + rr * sub_len:off + (rr + 1) * sub_len, :] = (
                        stage[pl.ds(rr, sub_len, stride=r), :].astype(dst_ref.dtype))
                return
            r1 = MAX_ROW_STRIDE // 2
            r2 = r // r1
            assert r % r1 == 0 and r2 <= MAX_ROW_STRIDE
            len1 = seq // r1
            for a in range(r1):
                lg[a * len1:(a + 1) * len1, :] = stage[pl.ds(a, len1, stride=r1), :]
            for rr in range(r):
                a, k = rr % r1, rr // r1
                dst_ref[off + rr * sub_len:off + (rr + 1) * sub_len, :] = (
                    lg[pl.ds(a * len1 + k, sub_len, stride=r2), :].astype(dst_ref.dtype))

        regroup(q_ref, qp, 0)
        regroup(k_ref, kp, SPAN)
        regroup(v_ref, og, 0)

        def vt_body(i, carry):
            base = pl.multiple_of(i * SPAN, SPAN)
            tr = og[pl.ds(base, SPAN), :].T.astype(BF16)
            vt2[i, :, SPAN:2 * SPAN] = tr
            vt2[i + 1, :, 0:SPAN] = tr
            return carry
        lax.fori_loop(0, n_blocks, vt_body, 0, unroll=DIL_VT_UNROLL)

        out_o, out_l = (o_run, l_run) if g == 0 else (og, lg)

        def blk_body(i, carry, nblk=nblk, out_o=out_o, out_l=out_l):
            base = pl.multiple_of(i * SPAN, SPAN)
            q = qp[pl.ds(base, SPAN), :]
            has_prev = jnp.broadcast_to((lax.rem(i, nblk) > 0).astype(jnp.int32), (2 * SPAN, SPAN)) > 0
            s = lax.dot_general(kp[pl.ds(base, 2 * SPAN), :], q, _NT_DIMS,
                                preferred_element_type=F32) * SCALE + jnp.where(has_prev, bias_both, bias_first)
            m = jnp.max(s, axis=0, keepdims=True)
            p = jnp.exp(s - m)
            den = jnp.sum(p, axis=0, keepdims=True)
            o_t = jnp.dot(vt2[i], p.astype(BF16), preferred_element_type=F32)
            o_t = o_t / jnp.maximum(den, TINY)
            lse = m + jnp.log(den)
            out_o[pl.ds(base, SPAN), :] = o_t.T
            out_l[pl.ds(base, SPAN), :] = jnp.broadcast_to(lse, (SPAN, SPAN)).T
            return carry
        lax.fori_loop(0, n_blocks, blk_body, 0, unroll=DIL_BLOCK_UNROLL)

        if g > 0:
            for rr in range(r):
                for c in range(sub_len // SPAN):
                    tok = pl.ds(rr + c * SPAN * r, SPAN, stride=r)
                    rows = pl.ds(rr * sub_len + c * SPAN, SPAN)
                    lp = l_run[tok, :]
                    ln = lg[rows, :]
                    mx = jnp.maximum(lp, ln)
                    wp = jnp.exp(lp - mx)
                    wn = jnp.exp(ln - mx)
                    tot = wp + wn
                    o_run[tok, :] = (o_run[tok, :] * wp + og[rows, :] * wn) / tot
                    if not last:
                        l_run[tok, :] = mx + jnp.log(tot)

    o_ref[0] = o_run[...].astype(o_ref.dtype)


def _dilated_attention(proj, batch, attn_w):
    t, e = proj.shape
    s = t // batch
    n_heads = attn_w // HEAD_DIM
    for window, r in DILATED_PAIRS:
        assert window // r == SPAN and s % (r * SPAN) == 0
    pv = proj.reshape(batch, s, e)

    def spec(g, c):
        return pl.BlockSpec((1, s, HEAD_DIM), lambda b, h: (b, 0, (g * 3 + c) * n_heads + h))

    seq_f32 = pltpu.VMEM((s, HEAD_DIM), F32)
    seq_bf16 = pltpu.VMEM((s, HEAD_DIM), BF16)
    out = pl.pallas_call(
        functools.partial(_dil_kernel, seq=s),
        grid=(batch, n_heads),
        in_specs=[spec(g, c) for g in range(len(DILATED_PAIRS)) for c in range(3)],
        out_specs=pl.BlockSpec((1, s, HEAD_DIM), lambda b, h: (b, 0, h)),
        out_shape=jax.ShapeDtypeStruct((batch, s, attn_w), BF16),
        scratch_shapes=[seq_f32, seq_bf16, pltpu.VMEM((s + SPAN, HEAD_DIM), BF16),
                        pltpu.VMEM((s // SPAN + 1, HEAD_DIM, 2 * SPAN), BF16),
                        seq_f32, seq_f32, seq_f32, seq_f32],
        compiler_params=_params("parallel", "parallel"),
        name="dilated_attn",
    )(*([pv] * 9))
    return out.reshape(t, attn_w)


MOBA_HEADS_PER_STEP = 2
MOBA_BLOCKS_PER_ITER = 4


def _moba_kernel(q_ref, k_ref, v_ref, o_ref, vt, kmean, kparts, sel_ref, acc_ref, *, seq):
    blk = MOBA_BLOCK
    nb = seq // blk
    nbp = kmean.shape[1]
    heads = range(MOBA_HEADS_PER_STEP)
    per_it = MOBA_BLOCKS_PER_ITER
    kmean[...] = jnp.zeros(kmean.shape, F32)
    sel_ref[...] = jnp.full(sel_ref.shape, NEG_INF, F32)

    def head_cols(h):
        return slice(h * HEAD_DIM, (h + 1) * HEAD_DIM)

    def pre(jb, carry):
        r0 = pl.multiple_of(jb * blk, blk)
        for h in heads:
            vt[h, jb] = v_ref[0, pl.ds(r0, blk), head_cols(h)].astype(F32).T.astype(BF16)
            kb = k_ref[0, pl.ds(r0, blk), head_cols(h)].astype(F32)
            kmean[h, pl.ds(jb, 1), :] = jnp.sum(kb, axis=0, keepdims=True) * (1.0 / blk)
        return carry
    lax.fori_loop(0, nb, pre, 0)

    for h in heads:
        km = kmean[h]
        hi = km.astype(BF16)
        r1 = km - hi.astype(F32)
        mid = r1.astype(BF16)
        lo = (r1 - mid.astype(F32)).astype(BF16)
        kparts[h, 0:nbp, :] = hi
        kparts[h, nbp:2 * nbp, :] = mid
        kparts[h, 2 * nbp:3 * nbp, :] = lo

    kblk = lax.broadcasted_iota(jnp.int32, (nbp, blk), 0)
    key = lax.broadcasted_iota(jnp.int32, (blk, blk), 0)
    qry = lax.broadcasted_iota(jnp.int32, (blk, blk), 1)
    bias_causal = jnp.where(key <= qry, 0.0, NEG_INF)

    def q_body(qi, carry):
        r0 = pl.multiple_of(qi * blk, blk)
        qs, ms, ls = [], [], []
        for h in heads:
            q = q_ref[0, pl.ds(r0, blk), head_cols(h)]
            g3 = lax.dot_general(kparts[h], q, _NT_DIMS, preferred_element_type=F32)
            gate = g3[0:nbp] + g3[nbp:2 * nbp] + g3[2 * nbp:3 * nbp]
            gate = jnp.where(kblk < qi, gate, NEG_INF)
            beaten = jnp.zeros((nbp, blk), F32)
            for jp in range(nb):
                gj = gate[jp:jp + 1, :]
                tie = jnp.where(kblk > jp, 1.0, 0.0)
                beaten = beaten + jnp.where(gj > gate, 1.0, jnp.where(gj == gate, tie, 0.0))
            sel_ref[h, 0:nbp, :] = jnp.where(beaten < MOBA_TOPK, jnp.where(kblk < qi, 0.0, NEG_INF), NEG_INF)

            s = lax.dot_general(k_ref[0, pl.ds(r0, blk), head_cols(h)], q, _NT_DIMS,
                                preferred_element_type=F32) * SCALE + bias_causal
            m = jnp.max(s, axis=0, keepdims=True)
            p = jnp.exp(s - m)
            acc_ref[h] = jnp.dot(vt[h, qi], p.astype(BF16), preferred_element_type=F32)
            qs.append(q)
            ms.append(m)
            ls.append(jnp.sum(p, axis=0, keepdims=True))

        def past(it, st):
            ms, ls = st
            new_m, new_l = [], []
            for h in heads:
                m, l = ms[h], ls[h]
                blocks = [it * per_it + u for u in range(per_it)]
                ss = []
                for jb in blocks:
                    c0 = pl.multiple_of(jnp.minimum(jb, nb - 1) * blk, blk)
                    ss.append(lax.dot_general(k_ref[0, pl.ds(c0, blk), head_cols(h)], qs[h], _NT_DIMS,
                                              preferred_element_type=F32) * SCALE + sel_ref[h, pl.ds(jb, 1), :])
                m_new = m
                for s in ss:
                    m_new = jnp.maximum(m_new, jnp.max(s, axis=0, keepdims=True))
                alpha = jnp.exp(m - m_new)
                l = alpha * l
                acc = alpha * acc_ref[h]
                for jb, s in zip(blocks, ss):
                    p = jnp.exp(s - m_new)
                    l = l + jnp.sum(p, axis=0, keepdims=True)
                    acc = acc + jnp.dot(vt[h, jnp.minimum(jb, nb - 1)], p.astype(BF16),
                                        preferred_element_type=F32)
                acc_ref[h] = acc
                new_m.append(m_new)
                new_l.append(l)
            return tuple(new_m), tuple(new_l)

        ms, ls = lax.fori_loop(0, (qi + per_it - 1) // per_it, past, (tuple(ms), tuple(ls)))
        for h in heads:
            o_ref[0, pl.ds(r0, blk), head_cols(h)] = (
                (acc_ref[h] / jnp.maximum(ls[h], TINY)).T.astype(o_ref.dtype))
        return carry

    lax.fori_loop(0, nb, q_body, 0)


def _moba_attention(proj, batch, attn_w):
    t, e = proj.shape
    s = t // batch
    assert s % MOBA_BLOCK == 0
    nb = s // MOBA_BLOCK
    nbp = -(-nb // BF16_ROWS) * BF16_ROWS
    n_heads = attn_w // HEAD_DIM
    hp = MOBA_HEADS_PER_STEP
    assert n_heads % hp == 0
    w = hp * HEAD_DIM
    pv = proj.reshape(batch, s, e)

    def spec(c):
        return pl.BlockSpec((1, s, w), lambda b, h: (b, 0, c * (n_heads // hp) + h))

    out = pl.pallas_call(
        functools.partial(_moba_kernel, seq=s),
        grid=(batch, n_heads // hp),
        in_specs=[spec(0), spec(1), spec(2)],
        out_specs=pl.BlockSpec((1, s, w), lambda b, h: (b, 0, h)),
        out_shape=jax.ShapeDtypeStruct((batch, s, attn_w), BF16),
        scratch_shapes=[pltpu.VMEM((hp, nb, HEAD_DIM, MOBA_BLOCK), BF16),
                        pltpu.VMEM((hp, nbp, HEAD_DIM), F32),
                        pltpu.VMEM((hp, 3 * nbp, HEAD_DIM), BF16),
                        pltpu.VMEM((hp, nbp + MOBA_BLOCKS_PER_ITER, MOBA_BLOCK), F32),
                        pltpu.VMEM((hp, HEAD_DIM, MOBA_BLOCK), F32)],
        compiler_params=_params("parallel", "parallel"),
        name="moba_attn",
    )(pv, pv, pv)
    return out.reshape(t, attn_w)


def _router_kernel(x_ref, g_ref, wr_ref, i1_ref, i2_ref, g1_ref, g2_ref, *, n_experts):
    y = _rms(x_ref[...], g_ref[...])
    lg = jnp.dot(y, wr_ref[...], preferred_element_type=F32, precision=lax.Precision.HIGHEST)
    lane = lax.broadcasted_iota(jnp.int32, lg.shape, 1).astype(F32)
    lg = jnp.where(lane < n_experts, lg, NEG_INF)
    m1 = jnp.max(lg, axis=1, keepdims=True)
    i1 = jnp.min(jnp.where(lg == m1, lane, float(LANES)), axis=1, keepdims=True)
    lg2 = jnp.where(lane == i1, NEG_INF, lg)
    m2 = jnp.max(lg2, axis=1, keepdims=True)
    i2 = jnp.min(jnp.where(lg2 == m2, lane, float(LANES)), axis=1, keepdims=True)
    e = jnp.exp(m2 - m1)
    i1_ref[...] = i1.astype(jnp.int32)
    i2_ref[...] = i2.astype(jnp.int32)
    g1_ref[...] = 1.0 / (1.0 + e)
    g2_ref[...] = e / (1.0 + e)


def _router(h, g, w_router):
    t, d = h.shape
    n_experts = w_router.shape[1]
    assert n_experts <= LANES
    tm = 256
    wr = jnp.zeros((d, LANES), F32).at[:, :n_experts].set(w_router)
    col = pl.BlockSpec((tm, 1), lambda i: (i, 0))
    return pl.pallas_call(
        functools.partial(_router_kernel, n_experts=n_experts),
        grid=(t // tm,),
        in_specs=[pl.BlockSpec((tm, d), lambda i: (i, 0)), pl.BlockSpec((1, d), lambda i: (0, 0)),
                  pl.BlockSpec((d, LANES), lambda i: (0, 0))],
        out_specs=[col, col, col, col],
        out_shape=[jax.ShapeDtypeStruct((t, 1), jnp.int32)] * 2 + [jax.ShapeDtypeStruct((t, 1), F32)] * 2,
        compiler_params=_params("parallel"),
        name="router",
    )(h, g.reshape(1, d), wr)


def _route_plan(i1, i2, n_experts, tm):
    t = i1.shape[0]
    e = jnp.concatenate([i1, i2])
    onehot = (e[:, None] == jnp.arange(n_experts, dtype=jnp.int32)[None, :]).astype(jnp.int32)
    csum = jnp.cumsum(onehot, axis=0)
    rank = jnp.take_along_axis(csum, e[:, None], axis=1)[:, 0] - 1
    counts = csum[-1]
    padded = ((counts + tm - 1) // tm) * tm
    ends = jnp.cumsum(padded)
    starts = ends - padded
    pos = starts[e] + rank
    n_rows = TOP_K * t + n_experts * tm
    tok = jnp.arange(t, dtype=jnp.int32)
    row_token = jnp.zeros((n_rows,), jnp.int32).at[pos].set(jnp.concatenate([tok, tok]))
    tile_start = jnp.arange(n_rows // tm, dtype=jnp.int32) * tm
    grp = jnp.sum((tile_start[:, None] >= ends[None, :]).astype(jnp.int32), axis=1)
    in_use = grp < n_experts
    grp = jnp.minimum(grp, n_experts - 1)
    nrows = jnp.where(in_use, jnp.clip((starts + counts)[grp] - tile_start, 0, tm), 0).astype(jnp.int32)
    first = jnp.concatenate([jnp.ones((1,), jnp.int32), (grp[1:] != grp[:-1]).astype(jnp.int32)])
    g_start = jnp.arange(n_rows // GATHER_ROWS, dtype=jnp.int32) * GATHER_ROWS
    g_valid = ((g_start % tm) < nrows[g_start // tm]).astype(jnp.int32)
    return (grp, first, nrows), row_token, g_valid, pos[:t], pos[t:]


GATHER_ROWS = 256
DMA_ISSUE_UNROLL = 8


def _gather_norm_kernel(tok_ref, valid_ref, g_ref, h_hbm, o_ref, buf, sem, *, tm):
    i = pl.program_id(0)

    def row_copy(r):
        return pltpu.make_async_copy(h_hbm.at[pl.ds(tok_ref[i * tm + r], 1)], buf.at[pl.ds(r, 1)], sem)

    @pl.when(valid_ref[i] == 0)
    def _skip():
        o_ref[...] = jnp.zeros(o_ref.shape, o_ref.dtype)

    @pl.when(valid_ref[i] == 1)
    def _gather():
        def start(r, carry):
            row_copy(r).start()
            return carry
        lax.fori_loop(0, tm, start, 0, unroll=DMA_ISSUE_UNROLL)

        def wait(r, carry):
            row_copy(r).wait()
            return carry
        lax.fori_loop(0, tm, wait, 0, unroll=DMA_ISSUE_UNROLL)
        o_ref[...] = _rms(buf[...], g_ref[...]).astype(o_ref.dtype)


def _gather_norm(h, g, row_token, valid, tm):
    t, d = h.shape
    n_rows = row_token.shape[0]
    grid_spec = pltpu.PrefetchScalarGridSpec(
        num_scalar_prefetch=2,
        grid=(n_rows // tm,),
        in_specs=[pl.BlockSpec((1, d), lambda i, tok, v: (0, 0)), pl.BlockSpec(memory_space=pl.ANY)],
        out_specs=pl.BlockSpec((tm, d), lambda i, tok, v: (i, 0)),
        scratch_shapes=[pltpu.VMEM((tm, d), F32), pltpu.SemaphoreType.DMA(())],
    )
    return pl.pallas_call(
        functools.partial(_gather_norm_kernel, tm=tm),
        grid_spec=grid_spec,
        out_shape=jax.ShapeDtypeStruct((n_rows, d), BF16),
        compiler_params=_params("arbitrary"),
        name="moe_gather_norm",
    )(row_token, valid, g.reshape(1, d), h)


def _combine_kernel(p1_ref, p2_ref, h_ref, g1_ref, g2_ref, fn_ref, y_hbm, o_ref, buf, sem, *, tm):
    i = pl.program_id(0)

    def row_copy(r, which, p_ref):
        return pltpu.make_async_copy(y_hbm.at[pl.ds(p_ref[i * tm + r], 1)], buf.at[which, pl.ds(r, 1)], sem)

    def start(r, carry):
        row_copy(r, 0, p1_ref).start()
        row_copy(r, 1, p2_ref).start()
        return carry
    lax.fori_loop(0, tm, start, 0, unroll=DMA_ISSUE_UNROLL)

    def wait(r, carry):
        row_copy(r, 0, p1_ref).wait()
        row_copy(r, 1, p2_ref).wait()
        return carry
    lax.fori_loop(0, tm, wait, 0, unroll=DMA_ISSUE_UNROLL)
    h = h_ref[...] + (g1_ref[...] * buf[0] + g2_ref[...] * buf[1])
    o_ref[...] = _rms(h, fn_ref[...])


def _combine_norm(h, y, pos1, pos2, g1, g2, final_norm):
    t, d = h.shape
    tm = 256
    col = pl.BlockSpec((tm, 1), lambda i, a, b: (i, 0))
    row = pl.BlockSpec((tm, d), lambda i, a, b: (i, 0))
    grid_spec = pltpu.PrefetchScalarGridSpec(
        num_scalar_prefetch=2,
        grid=(t // tm,),
        in_specs=[row, col, col, pl.BlockSpec((1, d), lambda i, a, b: (0, 0)),
                  pl.BlockSpec(memory_space=pl.ANY)],
        out_specs=row,
        scratch_shapes=[pltpu.VMEM((TOP_K, tm, d), F32), pltpu.SemaphoreType.DMA(())],
    )
    return pl.pallas_call(
        functools.partial(_combine_kernel, tm=tm),
        grid_spec=grid_spec,
        out_shape=jax.ShapeDtypeStruct((t, d), F32),
        compiler_params=_params("arbitrary"),
        name="moe_combine_norm",
    )(pos1, pos2, h, g1, g2, final_norm.reshape(1, d), y)


TM_DENSE = 1024
TN_DENSE = 512
TN_STREAM_ROPE = 1024
TN_STREAM_SWIGLU = 512
TM_EXPERT = 512
TN_SWIGLU_EXPERT = 512
TN_DOWN_EXPERT = 1024
DOWN_K_SPLITS = 4


def _down_proj(a, w, plan, res, tm, tn):
    k = w.shape[1]
    splits = DOWN_K_SPLITS if k > 8192 else 1
    kb = k // splits
    for ki in range(splits):
        res = _matmul(a, (w,), plan, mode="residual", tm=tm, tn=tn, out_dtype=F32,
                      k_blk=kb, k_idx=ki, extras=(res,))
    return res


def kernel(x, positions, mix_norm, ffn_norm, dil_w_in, dil_w_out, moba_w_in, moba_w_out,
           ffn_w_gate, ffn_w_up, ffn_w_down, router_w, exp_w_gate, exp_w_up, exp_w_down,
           final_norm):
    batch, s, d = x.shape
    t = batch * s
    attn_w = dil_w_out.shape[1]
    h = x.reshape(t, d)
    rope = _rope_tables(positions)
    dense = _dense_plan(t // TM_DENSE, TM_DENSE)

    hn = _rmsnorm(h, mix_norm[0])
    proj = _matmul_stream(hn, (dil_w_in,), mode="rope", tm=TM_DENSE, tn=TN_STREAM_ROPE, out_dtype=BF16,
                          extras=rope, qkv_width=attn_w)
    o = _dilated_attention(proj, batch, attn_w)
    h = _matmul(o, (dil_w_out,), dense, mode="residual", tm=TM_DENSE, tn=TN_DENSE, out_dtype=F32,
                extras=(h,))
    hn = _rmsnorm(h, ffn_norm[0])
    a = _matmul_stream(hn, (ffn_w_gate, ffn_w_up), mode="swiglu", tm=TM_DENSE, tn=TN_STREAM_SWIGLU,
                       out_dtype=BF16)
    h = _down_proj(a, ffn_w_down, dense, h, TM_DENSE, TN_DENSE)

    hn = _rmsnorm(h, mix_norm[1])
    proj = _matmul_stream(hn, (moba_w_in,), mode="rope", tm=TM_DENSE, tn=TN_STREAM_ROPE, out_dtype=BF16,
                          extras=rope, qkv_width=attn_w)
    o = _moba_attention(proj, batch, attn_w)
    h = _matmul(o, (moba_w_out,), dense, mode="residual", tm=TM_DENSE, tn=TN_DENSE, out_dtype=F32,
                extras=(h,))

    n_experts = router_w.shape[-1]
    i1, i2, g1, g2 = _router(h, ffn_norm[1], router_w[0])
    plan, row_token, g_valid, pos1, pos2 = _route_plan(i1[:, 0], i2[:, 0], n_experts, TM_EXPERT)
    xs = _gather_norm(h, ffn_norm[1], row_token, g_valid, GATHER_ROWS)
    a = _matmul(xs, (exp_w_gate[0], exp_w_up[0]), plan, mode="swiglu", tm=TM_EXPERT,
                tn=TN_SWIGLU_EXPERT, out_dtype=BF16)
    y = _matmul(a, (exp_w_down[0],), plan, mode="plain", tm=TM_EXPERT, tn=TN_DOWN_EXPERT, out_dtype=F32)
    out = _combine_norm(h, y, pos1, pos2, g1, g2, final_norm)
    return out.reshape(batch, s, d)
```

```python
import functools

import jax
import jax.numpy as jnp
from jax import lax
from jax.experimental import pallas as pl
from jax.experimental.pallas import tpu as pltpu

HEAD_DIM = 128
ROPE_DIM = HEAD_DIM // 4
ROPE_HALF = ROPE_DIM // 2
ROPE_THETA = 500000.0
SCALE = HEAD_DIM ** -0.5
NORM_EPS = 1e-5
TINY = 1e-30
DILATED_PAIRS = ((128, 1), (512, 4), (2048, 16))
MOBA_BLOCK = 256
MOBA_TOPK = 3
TOP_K = 2

LANES = 128
BF16_ROWS = 16
VMEM_LIMIT_BYTES = 56 * 2 ** 20

NEG_INF = float("-inf")
F32 = jnp.float32
BF16 = jnp.bfloat16

_NT_DIMS = (((1,), (1,)), ((), ()))


def _params(*sem):
    return pltpu.CompilerParams(dimension_semantics=sem, vmem_limit_bytes=VMEM_LIMIT_BYTES)


def _rope_table_kernel(pos_ref, invf_ref, c_ref, s1_ref, s2_ref):
    ang = pos_ref[...] * invf_ref[...]
    lane = lax.broadcasted_iota(jnp.int32, ang.shape, 1)
    c = jnp.cos(ang)
    s = jnp.sin(ang)
    c_ref[...] = jnp.where(lane < ROPE_DIM, c, 1.0)
    s1_ref[...] = jnp.where(lane < ROPE_HALF, -s, 0.0)
    s2_ref[...] = jnp.where(lane < ROPE_HALF, 0.0, jnp.where(lane < ROPE_DIM, s, 0.0))


def _rope_tables(positions):
    t = positions.size
    tm = 1024
    pos = positions.reshape(t, 1).astype(F32)
    inv_freq = jnp.power(ROPE_THETA, -jnp.arange(ROPE_HALF, dtype=F32) / ROPE_HALF)
    invf = jnp.zeros((1, LANES), F32).at[0, :ROPE_DIM].set(jnp.tile(inv_freq, 2))
    tab = jax.ShapeDtypeStruct((t, LANES), F32)
    spec = pl.BlockSpec((tm, LANES), lambda i: (i, 0))
    return pl.pallas_call(
        _rope_table_kernel,
        grid=(t // tm,),
        in_specs=[pl.BlockSpec((tm, 1), lambda i: (i, 0)), pl.BlockSpec((1, LANES), lambda i: (0, 0))],
        out_specs=[spec, spec, spec],
        out_shape=[tab, tab, tab],
        compiler_params=_params("parallel"),
        name="rope_tables",
    )(pos, invf)


def _rms(x, g):
    ms = jnp.mean(x * x, axis=-1, keepdims=True)
    return x * lax.rsqrt(ms + NORM_EPS) * g


def _rmsnorm_kernel(x_ref, g_ref, o_ref):
    o_ref[...] = _rms(x_ref[...], g_ref[...]).astype(o_ref.dtype)


def _rmsnorm(x, g, out_dtype=BF16):
    t, d = x.shape
    tm = 512
    return pl.pallas_call(
        _rmsnorm_kernel,
        grid=(t // tm,),
        in_specs=[pl.BlockSpec((tm, d), lambda i: (i, 0)), pl.BlockSpec((1, d), lambda i: (0, 0))],
        out_specs=pl.BlockSpec((tm, d), lambda i: (i, 0)),
        out_shape=jax.ShapeDtypeStruct((t, d), out_dtype),
        compiler_params=_params("parallel"),
        name="rmsnorm",
    )(x, g.reshape(1, d))


CAST_ROWS = 256
MM_ROW_CHUNK = 256


def _cast_weight(w_ref, wb_ref):
    def body(c, carry):
        r = pl.multiple_of(c * CAST_ROWS, CAST_ROWS)
        wb_ref[pl.ds(r, CAST_ROWS), :] = w_ref[0, pl.ds(r, CAST_ROWS), :].astype(wb_ref.dtype)
        return carry
    lax.fori_loop(0, wb_ref.shape[0] // CAST_ROWS, body, 0)


def _rope_rotate(acc, c, s1, s2):
    return (acc * c + pltpu.roll(acc, LANES - ROPE_HALF, 1) * s1
            + pltpu.roll(acc, ROPE_HALF, 1) * s2)


def _mm_kernel(grp_ref, first_ref, nrows_ref, x_ref, *rest, mode):
    del grp_ref
    i = pl.program_id(1)
    if mode == "swiglu":
        wg_ref, wu_ref, o_ref, wgb, wub = rest
        weights = ((wg_ref, wgb), (wu_ref, wub))
    elif mode == "residual":
        w_ref, res_ref, o_ref, wb = rest
        weights = ((w_ref, wb),)
    else:
        w_ref, o_ref, wb = rest
        weights = ((w_ref, wb),)

    @pl.when(first_ref[i] == 1)
    def _cast():
        for w, b in weights:
            _cast_weight(w, b)

    tm, tn = o_ref.shape

    def chunk(r0):
        rows = slice(r0, r0 + MM_ROW_CHUNK)
        x = x_ref[rows, :]
        if mode == "swiglu":
            g = jnp.dot(x, wgb[...], preferred_element_type=F32)
            u = jnp.dot(x, wub[...], preferred_element_type=F32)
            o_ref[rows, :] = (g * (1.0 / (1.0 + jnp.exp(-g))) * u).astype(o_ref.dtype)
            return
        acc = jnp.dot(x, wb[...], preferred_element_type=F32)
        if mode == "residual":
            o_ref[rows, :] = res_ref[rows, :] + acc
        else:
            o_ref[rows, :] = acc.astype(o_ref.dtype)

    @pl.when(nrows_ref[i] == 0)
    def _skip():
        o_ref[...] = jnp.zeros(o_ref.shape, o_ref.dtype)

    @pl.when(nrows_ref[i] > 0)
    def _compute():
        for r0 in range(0, tm, MM_ROW_CHUNK):
            chunk(r0)


def _dense_plan(n_tiles, tm):
    grp = jnp.zeros((n_tiles,), jnp.int32)
    first = jnp.zeros((n_tiles,), jnp.int32).at[0].set(1)
    nrows = jnp.full((n_tiles,), tm, jnp.int32)
    return grp, first, nrows


def _matmul(x, ws, plan, *, mode, tm, tn, out_dtype, k_blk=None, k_idx=0, extras=()):
    m = x.shape[0]
    _, k, n = ws[0].shape
    kb = k if k_blk is None else k_blk
    grp, first, nrows = plan
    x_spec = pl.BlockSpec((tm, kb), lambda j, i, g, f, v: (i, k_idx))
    w_spec = pl.BlockSpec((1, kb, tn), lambda j, i, g, f, v: (g[i], k_idx, j))
    o_spec = pl.BlockSpec((tm, tn), lambda j, i, g, f, v: (i, j))
    in_specs = [x_spec] + [w_spec] * len(ws)
    if mode == "residual":
        in_specs.append(o_spec)
    grid_spec = pltpu.PrefetchScalarGridSpec(
        num_scalar_prefetch=3,
        grid=(n // tn, m // tm),
        in_specs=in_specs,
        out_specs=o_spec,
        scratch_shapes=[pltpu.VMEM((kb, tn), BF16)] * len(ws),
    )
    return pl.pallas_call(
        functools.partial(_mm_kernel, mode=mode),
        grid_spec=grid_spec,
        out_shape=jax.ShapeDtypeStruct((m, n), out_dtype),
        compiler_params=_params("arbitrary", "arbitrary"),
        name="mm_" + mode,
    )(grp, first, nrows, x, *ws, *extras)


STREAM_CAST_ROWS = 128


def _mm_stream_kernel(x_ref, *rest, mode, qkv_width):
    j = pl.program_id(0)
    i = pl.program_id(1)
    if mode == "swiglu":
        wg_ref, wu_ref, o_ref, wgb, wub = rest
        weights = ((wg_ref, wgb), (wu_ref, wub))
    else:
        w_ref, c_ref, s1_ref, s2_ref, o_ref, wb = rest
        weights = ((w_ref, wb),)
    tm, tn = o_ref.shape
    k_chunk = weights[0][0].shape[1]
    nxt = lax.rem(j, 2)
    cur = 1 - nxt

    def cast_chunk():
        base = i * k_chunk
        for w, b in weights:
            for r0 in range(0, k_chunk, STREAM_CAST_ROWS):
                r = pl.multiple_of(base + r0, STREAM_CAST_ROWS)
                b[nxt, pl.ds(r, STREAM_CAST_ROWS), :] = w[0, r0:r0 + STREAM_CAST_ROWS, :].astype(BF16)

    def compute():
        for r0 in range(0, tm, MM_ROW_CHUNK):
            rows = slice(r0, r0 + MM_ROW_CHUNK)
            x = x_ref[rows, :]
            if mode == "swiglu":
                g = jnp.dot(x, wgb[cur], preferred_element_type=F32)
                u = jnp.dot(x, wub[cur], preferred_element_type=F32)
                o_ref[rows, :] = (g * (1.0 / (1.0 + jnp.exp(-g))) * u).astype(o_ref.dtype)
                continue
            acc = jnp.dot(x, wb[cur], preferred_element_type=F32)
            is_v = ((((j - 1) * tn) // qkv_width) % 3 == 2).astype(jnp.int32)
            keep = jnp.broadcast_to(is_v, (MM_ROW_CHUNK, HEAD_DIM)) > 0
            c, s1, s2 = c_ref[rows, :], s1_ref[rows, :], s2_ref[rows, :]
            for hh in range(tn // HEAD_DIM):
                cs = slice(hh * HEAD_DIM, (hh + 1) * HEAD_DIM)
                a = acc[:, cs]
                o_ref[rows, cs] = jnp.where(keep, a, _rope_rotate(a, c, s1, s2)).astype(o_ref.dtype)

    @pl.when(j == 0)
    def _load_only():
        cast_chunk()
        o_ref[...] = jnp.zeros(o_ref.shape, o_ref.dtype)

    @pl.when(j > 0)
    def _load_and_multiply():
        cast_chunk()
        compute()


def _matmul_stream(x, ws, *, mode, tm, tn, out_dtype, extras=(), qkv_width=0):
    m, k = x.shape
    n = ws[0].shape[2]
    m_tiles = m // tm
    n_tiles = n // tn
    k_chunk = k // m_tiles
    assert k_chunk * m_tiles == k and k_chunk % STREAM_CAST_ROWS == 0
    x_spec = pl.BlockSpec((tm, k), lambda j, i: (i, 0))
    w_spec = pl.BlockSpec((1, k_chunk, tn), lambda j, i: (0, i, jnp.minimum(j, n_tiles - 1)))
    o_spec = pl.BlockSpec((tm, tn), lambda j, i: (i, jnp.where(j == 0, n_tiles, j - 1)))
    in_specs = [x_spec] + [w_spec] * len(ws)
    if mode == "rope":
        in_specs += [pl.BlockSpec((tm, LANES), lambda j, i: (i, 0))] * 3
    return pl.pallas_call(
        functools.partial(_mm_stream_kernel, mode=mode, qkv_width=qkv_width),
        grid=(n_tiles + 1, m_tiles),
        in_specs=in_specs,
        out_specs=o_spec,
        out_shape=jax.ShapeDtypeStruct((m, n + tn), out_dtype),
        scratch_shapes=[pltpu.VMEM((2, k, tn), BF16)] * len(ws),
        compiler_params=_params("arbitrary", "arbitrary"),
        name="mms_" + mode,
    )(x, *ws, *extras)


SPAN = 128
DIL_BLOCK_UNROLL = 32
DIL_VT_UNROLL = 8
MAX_ROW_STRIDE = 8


def _dil_kernel(*refs, seq):
    qkv = refs[:9]
    o_ref = refs[9]
    stage, qp, kp, vt2, og, lg, o_run, l_run = refs[10:]
    n_blocks = seq // SPAN

    key2 = lax.broadcasted_iota(jnp.int32, (2 * SPAN, SPAN), 0)
    qry2 = lax.broadcasted_iota(jnp.int32, (2 * SPAN, SPAN), 1)
    in_cur = jnp.where(key2 >= SPAN, jnp.where(key2 - SPAN <= qry2, 0.0, NEG_INF), NEG_INF)
    bias_first = in_cur
    bias_both = jnp.where(key2 < SPAN, jnp.where(key2 >= qry2, 0.0, NEG_INF), in_cur)

    kp[0:SPAN, :] = jnp.zeros((SPAN, HEAD_DIM), BF16)
    vt2[0, :, 0:SPAN] = jnp.zeros((HEAD_DIM, SPAN), BF16)

    for g, (window, r) in enumerate(DILATED_PAIRS):
        q_ref, k_ref, v_ref = qkv[3 * g:3 * g + 3]
        sub_len = seq // r
        nblk = sub_len // SPAN
        last = g == len(DILATED_PAIRS) - 1

        def regroup(src_ref, dst_ref, off, r=r, sub_len=sub_len):
            if r == 1:
                dst_ref[off:off + seq, :] = src_ref[0].astype(dst_ref.dtype)
                return
            stage[...] = src_ref[0].astype(F32)
            if r <= MAX_ROW_STRIDE:
                for rr in range(r):
                    dst_ref[off + rr * sub_len:off + (rr + 1) * sub_len, :] = (
                        stage[pl.ds(rr, sub_len, stride=r), :].astype(dst_ref.dtype))
                return
            r1 = MAX_ROW_STRIDE // 2
            r2 = r // r1
            assert r % r1 == 0 and r2 <= MAX_ROW_STRIDE
            len1 = seq // r1
            for a in range(r1):
                lg[a * len1:(a + 1) * len1, :] = stage[pl.ds(a, len1, stride=r1), :]
            for rr in range(r):
                a, k = rr % r1, rr // r1
                dst_ref[off + rr * sub_len:off + (rr + 1) * sub_len, :] = (
                    lg[pl.ds(a * len1 + k, sub_len, stride=r2), :].astype(dst_ref.dtype))

        regroup(q_ref, qp, 0)
        regroup(k_ref, kp, SPAN)
        regroup(v_ref, og, 0)

        def vt_body(i, carry):
            base = pl.multiple_of(i * SPAN, SPAN)
            tr = og[pl.ds(base, SPAN), :].T.astype(BF16)
            vt2[i, :, SPAN:2 * SPAN] = tr
            vt2[i + 1, :, 0:SPAN] = tr
            return carry
        lax.fori_loop(0, n_blocks, vt_body, 0, unroll=DIL_VT_UNROLL)

        out_o, out_l = (o_run, l_run) if g == 0 else (og, lg)

        def blk_body(i, carry, nblk=nblk, out_o=out_o, out_l=out_l):
            base = pl.multiple_of(i * SPAN, SPAN)
            q = qp[pl.ds(base, SPAN), :]
            has_prev = jnp.broadcast_to((lax.rem(i, nblk) > 0).astype(jnp.int32), (2 * SPAN, SPAN)) > 0
            s = lax.dot_general(kp[pl.ds(base, 2 * SPAN), :], q, _NT_DIMS,
                                preferred_element_type=F32) * SCALE + jnp.where(has_prev, bias_both, bias_first)
            m = jnp.max(s, axis=0, keepdims=True)
            p = jnp.exp(s - m)
            den = jnp.sum(p, axis=0, keepdims=True)
            o_t = jnp.dot(vt2[i], p.astype(BF16), preferred_element_type=F32)
            o_t = o_t / jnp.maximum(den, TINY)
            lse = m + jnp.log(den)
            out_o[pl.ds(base, SPAN), :] = o_t.T
            out_l[pl.ds(base, SPAN), :] = jnp.broadcast_to(lse, (SPAN, SPAN)).T
            return carry
        lax.fori_loop(0, n_blocks, blk_body, 0, unroll=DIL_BLOCK_UNROLL)

        if g > 0:
            for rr in range(r):
                for c in range(sub_len // SPAN):
                    tok = pl.ds(rr + c * SPAN * r, SPAN, stride=r)
                    rows = pl.ds(rr * sub_len + c * SPAN, SPAN)
                    lp = l_run[tok, :]
                    ln = lg[rows, :]
                    mx = jnp.maximum(lp, ln)
                    wp = jnp.exp(lp - mx)
                    wn = jnp.exp(ln - mx)
                    tot = wp + wn
                    o_run[tok, :] = (o_run[tok, :] * wp + og[rows, :] * wn) / tot
                    if not last:
                        l_run[tok, :] = mx + jnp.log(tot)

    o_ref[0] = o_run[...].astype(o_ref.dtype)


def _dilated_attention(proj, batch, attn_w):
    t, e = proj.shape
    s = t // batch
    n_heads = attn_w // HEAD_DIM
    for window, r in DILATED_PAIRS:
        assert window // r == SPAN and s % (r * SPAN) == 0
    pv = proj.reshape(batch, s, e)

    def spec(g, c):
        return pl.BlockSpec((1, s, HEAD_DIM), lambda b, h: (b, 0, (g * 3 + c) * n_heads + h))

    seq_f32 = pltpu.VMEM((s, HEAD_DIM), F32)
    seq_bf16 = pltpu.VMEM((s, HEAD_DIM), BF16)
    out = pl.pallas_call(
        functools.partial(_dil_kernel, seq=s),
        grid=(batch, n_heads),
        in_specs=[spec(g, c) for g in range(len(DILATED_PAIRS)) for c in range(3)],
        out_specs=pl.BlockSpec((1, s, HEAD_DIM), lambda b, h: (b, 0, h)),
        out_shape=jax.ShapeDtypeStruct((batch, s, attn_w), BF16),
        scratch_shapes=[seq_f32, seq_bf16, pltpu.VMEM((s + SPAN, HEAD_DIM), BF16),
                        pltpu.VMEM((s // SPAN + 1, HEAD_DIM, 2 * SPAN), BF16),
                        seq_f32, seq_f32, seq_f32, seq_f32],
        compiler_params=_params("parallel", "parallel"),
        name="dilated_attn",
    )(*([pv] * 9))
    return out.reshape(t, attn_w)


MOBA_HEADS_PER_STEP = 4
MOBA_BLOCKS_PER_ITER = 4


def _moba_kernel(q_ref, k_ref, v_ref, o_ref, vt, kmean, kparts, sel_ref, acc_ref, *, seq):
    blk = MOBA_BLOCK
    nb = seq // blk
    nbp = kmean.shape[1]
    heads = range(MOBA_HEADS_PER_STEP)
    per_it = MOBA_BLOCKS_PER_ITER
    kmean[...] = jnp.zeros(kmean.shape, F32)
    sel_ref[...] = jnp.full(sel_ref.shape, NEG_INF, F32)

    def head_cols(h):
        return slice(h * HEAD_DIM, (h + 1) * HEAD_DIM)

    def pre(jb, carry):
        r0 = pl.multiple_of(jb * blk, blk)
        for h in heads:
            vt[h, jb] = v_ref[0, pl.ds(r0, blk), head_cols(h)].astype(F32).T.astype(BF16)
            kb = k_ref[0, pl.ds(r0, blk), head_cols(h)].astype(F32)
            kmean[h, pl.ds(jb, 1), :] = jnp.sum(kb, axis=0, keepdims=True) * (1.0 / blk)
        return carry
    lax.fori_loop(0, nb, pre, 0)

    for h in heads:
        km = kmean[h]
        hi = km.astype(BF16)
        r1 = km - hi.astype(F32)
        mid = r1.astype(BF16)
        lo = (r1 - mid.astype(F32)).astype(BF16)
        kparts[h, 0:nbp, :] = hi
        kparts[h, nbp:2 * nbp, :] = mid
        kparts[h, 2 * nbp:3 * nbp, :] = lo

    kblk = lax.broadcasted_iota(jnp.int32, (nbp, blk), 0)
    key = lax.broadcasted_iota(jnp.int32, (blk, blk), 0)
    qry = lax.broadcasted_iota(jnp.int32, (blk, blk), 1)
    bias_causal = jnp.where(key <= qry, 0.0, NEG_INF)

    def q_body(qi, carry):
        r0 = pl.multiple_of(qi * blk, blk)
        qs, ms, ls = [], [], []
        for h in heads:
            q = q_ref[0, pl.ds(r0, blk), head_cols(h)]
            g3 = lax.dot_general(kparts[h], q, _NT_DIMS, preferred_element_type=F32)
            gate = g3[0:nbp] + g3[nbp:2 * nbp] + g3[2 * nbp:3 * nbp]
            gate = jnp.where(kblk < qi, gate, NEG_INF)
            beaten = jnp.zeros((nbp, blk), F32)
            for jp in range(nb):
                gj = gate[jp:jp + 1, :]
                tie = jnp.where(kblk > jp, 1.0, 0.0)
                beaten = beaten + jnp.where(gj > gate, 1.0, jnp.where(gj == gate, tie, 0.0))
            sel_ref[h, 0:nbp, :] = jnp.where(beaten < MOBA_TOPK, jnp.where(kblk < qi, 0.0, NEG_INF), NEG_INF)

            s = lax.dot_general(k_ref[0, pl.ds(r0, blk), head_cols(h)], q, _NT_DIMS,
                                preferred_element_type=F32) * SCALE + bias_causal
            m = jnp.max(s, axis=0, keepdims=True)
            p = jnp.exp(s - m)
            acc_ref[h] = jnp.dot(vt[h, qi], p.astype(BF16), preferred_element_type=F32)
            qs.append(q)
            ms.append(m)
            ls.append(jnp.sum(p, axis=0, keepdims=True))

        def past(it, st):
            ms, ls = st
            new_m, new_l = [], []
            for h in heads:
                m, l = ms[h], ls[h]
                blocks = [it * per_it + u for u in range(per_it)]
                ss = []
                for jb in blocks:
                    c0 = pl.multiple_of(jnp.minimum(jb, nb - 1) * blk, blk)
                    ss.append(lax.dot_general(k_ref[0, pl.ds(c0, blk), head_cols(h)], qs[h], _NT_DIMS,
                                              preferred_element_type=F32) * SCALE + sel_ref[h, pl.ds(jb, 1), :])
                m_new = m
                for s in ss:
                    m_new = jnp.maximum(m_new, jnp.max(s, axis=0, keepdims=True))
                alpha = jnp.exp(m - m_new)
                l = alpha * l
                acc = alpha * acc_ref[h]
                for jb, s in zip(blocks, ss):
                    p = jnp.exp(s - m_new)
                    l = l + jnp.sum(p, axis=0, keepdims=True)
                    acc = acc + jnp.dot(vt[h, jnp.minimum(jb, nb - 1)], p.astype(BF16),
                                        preferred_element_type=F32)
                acc_ref[h] = acc
                new_m.append(m_new)
                new_l.append(l)
            return tuple(new_m), tuple(new_l)

        ms, ls = lax.fori_loop(0, (qi + per_it - 1) // per_it, past, (tuple(ms), tuple(ls)))
        for h in heads:
            o_ref[0, pl.ds(r0, blk), head_cols(h)] = (
                (acc_ref[h] / jnp.maximum(ls[h], TINY)).T.astype(o_ref.dtype))
        return carry

    lax.fori_loop(0, nb, q_body, 0)


def _moba_attention(proj, batch, attn_w):
    t, e = proj.shape
    s = t // batch
    assert s % MOBA_BLOCK == 0
    nb = s // MOBA_BLOCK
    nbp = -(-nb // BF16_ROWS) * BF16_ROWS
    n_heads = attn_w // HEAD_DIM
    hp = MOBA_HEADS_PER_STEP
    assert n_heads % hp == 0
    w = hp * HEAD_DIM
    pv = proj.reshape(batch, s, e)

    def spec(c):
        return pl.BlockSpec((1, s, w), lambda b, h: (b, 0, c * (n_heads // hp) + h))

    out = pl.pallas_call(
        functools.partial(_moba_kernel, seq=s),
        grid=(batch, n_heads // hp),
        in_specs=[spec(0), spec(1), spec(2)],
        out_specs=pl.BlockSpec((1, s, w), lambda b, h: (b, 0, h)),
        out_shape=jax.ShapeDtypeStruct((batch, s, attn_w), BF16),
        scratch_shapes=[pltpu.VMEM((hp, nb, HEAD_DIM, MOBA_BLOCK), BF16),
                        pltpu.VMEM((hp, nbp, HEAD_DIM), F32),
                        pltpu.VMEM((hp, 3 * nbp, HEAD_DIM), BF16),
                        pltpu.VMEM((hp, nbp + MOBA_BLOCKS_PER_ITER, MOBA_BLOCK), F32),
                        pltpu.VMEM((hp, HEAD_DIM, MOBA_BLOCK), F32)],
        compiler_params=_params("parallel", "parallel"),
        name="moba_attn",
    )(pv, pv, pv)
    return out.reshape(t, attn_w)


def _router_kernel(x_ref, g_ref, wr_ref, i1_ref, i2_ref, g1_ref, g2_ref, *, n_experts):
    y = _rms(x_ref[...], g_ref[...])
    lg = jnp.dot(y, wr_ref[...], preferred_element_type=F32, precision=lax.Precision.HIGHEST)
    lane = lax.broadcasted_iota(jnp.int32, lg.shape, 1).astype(F32)
    lg = jnp.where(lane < n_experts, lg, NEG_INF)
    m1 = jnp.max(lg, axis=1, keepdims=True)
    i1 = jnp.min(jnp.where(lg == m1, lane, float(LANES)), axis=1, keepdims=True)
    lg2 = jnp.where(lane == i1, NEG_INF, lg)
    m2 = jnp.max(lg2, axis=1, keepdims=True)
    i2 = jnp.min(jnp.where(lg2 == m2, lane, float(LANES)), axis=1, keepdims=True)
    e = jnp.exp(m2 - m1)
    i1_ref[...] = i1.astype(jnp.int32)
    i2_ref[...] = i2.astype(jnp.int32)
    g1_ref[...] = 1.0 / (1.0 + e)
    g2_ref[...] = e / (1.0 + e)


def _router(h, g, w_router):
    t, d = h.shape
    n_experts = w_router.shape[1]
    assert n_experts <= LANES
    tm = 256
    wr = jnp.zeros((d, LANES), F32).at[:, :n_experts].set(w_router)
    col = pl.BlockSpec((tm, 1), lambda i: (i, 0))
    return pl.pallas_call(
        functools.partial(_router_kernel, n_experts=n_experts),
        grid=(t // tm,),
        in_specs=[pl.BlockSpec((tm, d), lambda i: (i, 0)), pl.BlockSpec((1, d), lambda i: (0, 0)),
                  pl.BlockSpec((d, LANES), lambda i: (0, 0))],
        out_specs=[col, col, col, col],
        out_shape=[jax.ShapeDtypeStruct((t, 1), jnp.int32)] * 2 + [jax.ShapeDtypeStruct((t, 1), F32)] * 2,
        compiler_params=_params("parallel"),
        name="router",
    )(h, g.reshape(1, d), wr)


def _route_plan(i1, i2, n_experts, tm):
    t = i1.shape[0]
    e = jnp.concatenate([i1, i2])
    onehot = (e[:, None] == jnp.arange(n_experts, dtype=jnp.int32)[None, :]).astype(jnp.int32)
    csum = jnp.cumsum(onehot, axis=0)
    rank = jnp.take_along_axis(csum, e[:, None], axis=1)[:, 0] - 1
    counts = csum[-1]
    padded = ((counts + tm - 1) // tm) * tm
    ends = jnp.cumsum(padded)
    starts = ends - padded
    pos = starts[e] + rank
    n_rows = TOP_K * t + n_experts * tm
    tok = jnp.arange(t, dtype=jnp.int32)
    row_token = jnp.zeros((n_rows,), jnp.int32).at[pos].set(jnp.concatenate([tok, tok]))
    tile_start = jnp.arange(n_rows // tm, dtype=jnp.int32) * tm
    grp = jnp.sum((tile_start[:, None] >= ends[None, :]).astype(jnp.int32), axis=1)
    in_use = grp < n_experts
    grp = jnp.minimum(grp, n_experts - 1)
    nrows = jnp.where(in_use, jnp.clip((starts + counts)[grp] - tile_start, 0, tm), 0).astype(jnp.int32)
    first = jnp.concatenate([jnp.ones((1,), jnp.int32), (grp[1:] != grp[:-1]).astype(jnp.int32)])
    g_start = jnp.arange(n_rows // GATHER_ROWS, dtype=jnp.int32) * GATHER_ROWS
    g_valid = ((g_start % tm) < nrows[g_start // tm]).astype(jnp.int32)
    return (grp, first, nrows), row_token, g_valid, pos[:t], pos[t:]


GATHER_ROWS = 256
DMA_ISSUE_UNROLL = 8


def _gather_norm_kernel(tok_ref, valid_ref, g_ref, h_hbm, o_ref, buf, sem, *, tm):
    i = pl.program_id(0)

    def row_copy(r):
        return pltpu.make_async_copy(h_hbm.at[pl.ds(tok_ref[i * tm + r], 1)], buf.at[pl.ds(r, 1)], sem)

    @pl.when(valid_ref[i] == 0)
    def _skip():
        o_ref[...] = jnp.zeros(o_ref.shape, o_ref.dtype)

    @pl.when(valid_ref[i] == 1)
    def _gather():
        def start(r, carry):
            row_copy(r).start()
            return carry
        lax.fori_loop(0, tm, start, 0, unroll=DMA_ISSUE_UNROLL)

        def wait(r, carry):
            row_copy(r).wait()
            return carry
        lax.fori_loop(0, tm, wait, 0, unroll=DMA_ISSUE_UNROLL)
        o_ref[...] = _rms(buf[...], g_ref[...]).astype(o_ref.dtype)


def _gather_norm(h, g, row_token, valid, tm):
    t, d = h.shape
    n_rows = row_token.shape[0]
    grid_spec = pltpu.PrefetchScalarGridSpec(
        num_scalar_prefetch=2,
        grid=(n_rows // tm,),
        in_specs=[pl.BlockSpec((1, d), lambda i, tok, v: (0, 0)), pl.BlockSpec(memory_space=pl.ANY)],
        out_specs=pl.BlockSpec((tm, d), lambda i, tok, v: (i, 0)),
        scratch_shapes=[pltpu.VMEM((tm, d), F32), pltpu.SemaphoreType.DMA(())],
    )
    return pl.pallas_call(
        functools.partial(_gather_norm_kernel, tm=tm),
        grid_spec=grid_spec,
        out_shape=jax.ShapeDtypeStruct((n_rows, d), BF16),
        compiler_params=_params("arbitrary"),
        name="moe_gather_norm",
    )(row_token, valid, g.reshape(1, d), h)


def _combine_kernel(p1_ref, p2_ref, h_ref, g1_ref, g2_ref, fn_ref, y_hbm, o_ref, buf, sem, *, tm):
    i = pl.program_id(0)

    def row_copy(r, which, p_ref):
        return pltpu.make_async_copy(y_hbm.at[pl.ds(p_ref[i * tm + r], 1)], buf.at[which, pl.ds(r, 1)], sem)

    def start(r, carry):
        row_copy(r, 0, p1_ref).start()
        row_copy(r, 1, p2_ref).start()
        return carry
    lax.fori_loop(0, tm, start, 0, unroll=DMA_ISSUE_UNROLL)

    def wait(r, carry):
        row_copy(r, 0, p1_ref).wait()
        row_copy(r, 1, p2_ref).wait()
        return carry
    lax.fori_loop(0, tm, wait, 0, unroll=DMA_ISSUE_UNROLL)
    h = h_ref[...] + (g1_ref[...] * buf[0] + g2_ref[...] * buf[1])
    o_ref[...] = _rms(h, fn_ref[...])


def _combine_norm(h, y, pos1, pos2, g1, g2, final_norm):
    t, d = h.shape
    tm = 256
    col = pl.BlockSpec((tm, 1), lambda i, a, b: (i, 0))
    row = pl.BlockSpec((tm, d), lambda i, a, b: (i, 0))
    grid_spec = pltpu.PrefetchScalarGridSpec(
        num_scalar_prefetch=2,
        grid=(t // tm,),
        in_specs=[row, col, col, pl.BlockSpec((1, d), lambda i, a, b: (0, 0)),
                  pl.BlockSpec(memory_space=pl.ANY)],
        out_specs=row,
        scratch_shapes=[pltpu.VMEM((TOP_K, tm, d), F32), pltpu.SemaphoreType.DMA(())],
    )
    return pl.pallas_call(
        functools.partial(_combine_kernel, tm=tm),
        grid_spec=grid_spec,
        out_shape=jax.ShapeDtypeStruct((t, d), F32),
        compiler_params=_params("arbitrary"),
        name="moe_combine_norm",
    )(pos1, pos2, h, g1, g2, final_norm.reshape(1, d), y)


TM_DENSE = 1024
TN_DENSE = 512
TN_STREAM_ROPE = 1024
TN_STREAM_SWIGLU = 512
TM_EXPERT = 512
TN_SWIGLU_EXPERT = 512
TN_DOWN_EXPERT = 1024
DOWN_K_SPLITS = 4


def _down_proj(a, w, plan, res, tm, tn):
    k = w.shape[1]
    splits = DOWN_K_SPLITS if k > 8192 else 1
    kb = k // splits
    for ki in range(splits):
        res = _matmul(a, (w,), plan, mode="residual", tm=tm, tn=tn, out_dtype=F32,
                      k_blk=kb, k_idx=ki, extras=(res,))
    return res


def kernel(x, positions, mix_norm, ffn_norm, dil_w_in, dil_w_out, moba_w_in, moba_w_out,
           ffn_w_gate, ffn_w_up, ffn_w_down, router_w, exp_w_gate, exp_w_up, exp_w_down,
           final_norm):
    batch, s, d = x.shape
    t = batch * s
    attn_w = dil_w_out.shape[1]
    h = x.reshape(t, d)
    rope = _rope_tables(positions)
    dense = _dense_plan(t // TM_DENSE, TM_DENSE)

    hn = _rmsnorm(h, mix_norm[0])
    proj = _matmul_stream(hn, (dil_w_in,), mode="rope", tm=TM_DENSE, tn=TN_STREAM_ROPE, out_dtype=BF16,
                          extras=rope, qkv_width=attn_w)
    o = _dilated_attention(proj, batch, attn_w)
    h = _matmul(o, (dil_w_out,), dense, mode="residual", tm=TM_DENSE, tn=TN_DENSE, out_dtype=F32,
                extras=(h,))
    hn = _rmsnorm(h, ffn_norm[0])
    a = _matmul_stream(hn, (ffn_w_gate, ffn_w_up), mode="swiglu", tm=TM_DENSE, tn=TN_STREAM_SWIGLU,
                       out_dtype=BF16)
    h = _down_proj(a, ffn_w_down, dense, h, TM_DENSE, TN_DENSE)

    hn = _rmsnorm(h, mix_norm[1])
    proj = _matmul_stream(hn, (moba_w_in,), mode="rope", tm=TM_DENSE, tn=TN_STREAM_ROPE, out_dtype=BF16,
                          extras=rope, qkv_width=attn_w)
    o = _moba_attention(proj, batch, attn_w)
    h = _matmul(o, (moba_w_out,), dense, mode="residual", tm=TM_DENSE, tn=TN_DENSE, out_dtype=F32,
                extras=(h,))

    n_experts = router_w.shape[-1]
    i1, i2, g1, g2 = _router(h, ffn_norm[1], router_w[0])
    plan, row_token, g_valid, pos1, pos2 = _route_plan(i1[:, 0], i2[:, 0], n_experts, TM_EXPERT)
    xs = _gather_norm(h, ffn_norm[1], row_token, g_valid, GATHER_ROWS)
    a = _matmul(xs, (exp_w_gate[0], exp_w_up[0]), plan, mode="swiglu", tm=TM_EXPERT,
                tn=TN_SWIGLU_EXPERT, out_dtype=BF16)
    y = _matmul(a, (exp_w_down[0],), plan, mode="plain", tm=TM_EXPERT, tn=TN_DOWN_EXPERT, out_dtype=F32)
    out = _combine_norm(h, y, pos1, pos2, g1, g2, final_norm)
    return out.reshape(batch, s, d)
```

```python
import functools

import jax
import jax.numpy as jnp
from jax import lax
from jax.experimental import pallas as pl
from jax.experimental.pallas import tpu as pltpu

HEAD_DIM = 128
ROPE_DIM = HEAD_DIM // 4
ROPE_HALF = ROPE_DIM // 2
ROPE_THETA = 500000.0
SCALE = HEAD_DIM ** -0.5
NORM_EPS = 1e-5
TINY = 1e-30
DILATED_PAIRS = ((128, 1), (512, 4), (2048, 16))
MOBA_BLOCK = 256
MOBA_TOPK = 3
TOP_K = 2

LANES = 128
BF16_ROWS = 16
VMEM_LIMIT_BYTES = 56 * 2 ** 20

NEG_INF = float("-inf")
F32 = jnp.float32
BF16 = jnp.bfloat16

_NT_DIMS = (((1,), (1,)), ((), ()))


def _params(*sem):
    return pltpu.CompilerParams(dimension_semantics=sem, vmem_limit_bytes=VMEM_LIMIT_BYTES)


def _rope_table_kernel(pos_ref, invf_ref, c_ref, s1_ref, s2_ref):
    ang = pos_ref[...] * invf_ref[...]
    lane = lax.broadcasted_iota(jnp.int32, ang.shape, 1)
    c = jnp.cos(ang)
    s = jnp.sin(ang)
    c_ref[...] = jnp.where(lane < ROPE_DIM, c, 1.0)
    s1_ref[...] = jnp.where(lane < ROPE_HALF, -s, 0.0)
    s2_ref[...] = jnp.where(lane < ROPE_HALF, 0.0, jnp.where(lane < ROPE_DIM, s, 0.0))


def _rope_tables(positions):
    t = positions.size
    tm = 1024
    pos = positions.reshape(t, 1).astype(F32)
    inv_freq = jnp.power(ROPE_THETA, -jnp.arange(ROPE_HALF, dtype=F32) / ROPE_HALF)
    invf = jnp.zeros((1, LANES), F32).at[0, :ROPE_DIM].set(jnp.tile(inv_freq, 2))
    tab = jax.ShapeDtypeStruct((t, LANES), F32)
    spec = pl.BlockSpec((tm, LANES), lambda i: (i, 0))
    return pl.pallas_call(
        _rope_table_kernel,
        grid=(t // tm,),
        in_specs=[pl.BlockSpec((tm, 1), lambda i: (i, 0)), pl.BlockSpec((1, LANES), lambda i: (0, 0))],
        out_specs=[spec, spec, spec],
        out_shape=[tab, tab, tab],
        compiler_params=_params("parallel"),
        name="rope_tables",
    )(pos, invf)


def _rms(x, g):
    ms = jnp.mean(x * x, axis=-1, keepdims=True)
    return x * lax.rsqrt(ms + NORM_EPS) * g


def _rmsnorm_kernel(x_ref, g_ref, o_ref):
    o_ref[...] = _rms(x_ref[...], g_ref[...]).astype(o_ref.dtype)


def _rmsnorm(x, g, out_dtype=BF16):
    t, d = x.shape
    tm = 512
    return pl.pallas_call(
        _rmsnorm_kernel,
        grid=(t // tm,),
        in_specs=[pl.BlockSpec((tm, d), lambda i: (i, 0)), pl.BlockSpec((1, d), lambda i: (0, 0))],
        out_specs=pl.BlockSpec((tm, d), lambda i: (i, 0)),
        out_shape=jax.ShapeDtypeStruct((t, d), out_dtype),
        compiler_params=_params("parallel"),
        name="rmsnorm",
    )(x, g.reshape(1, d))


CAST_ROWS = 256
MM_ROW_CHUNK = 256


def _cast_weight(w_ref, wb_ref):
    def body(c, carry):
        r = pl.multiple_of(c * CAST_ROWS, CAST_ROWS)
        wb_ref[pl.ds(r, CAST_ROWS), :] = w_ref[0, pl.ds(r, CAST_ROWS), :].astype(wb_ref.dtype)
        return carry
    lax.fori_loop(0, wb_ref.shape[0] // CAST_ROWS, body, 0)


def _rope_rotate(acc, c, s1, s2):
    return (acc * c + pltpu.roll(acc, LANES - ROPE_HALF, 1) * s1
            + pltpu.roll(acc, ROPE_HALF, 1) * s2)


def _mm_kernel(grp_ref, first_ref, nrows_ref, x_ref, *rest, mode):
    del grp_ref
    i = pl.program_id(1)
    if mode == "swiglu":
        wg_ref, wu_ref, o_ref, wgb, wub = rest
        weights = ((wg_ref, wgb), (wu_ref, wub))
    elif mode == "residual":
        w_ref, res_ref, o_ref, wb = rest
        weights = ((w_ref, wb),)
    else:
        w_ref, o_ref, wb = rest
        weights = ((w_ref, wb),)

    @pl.when(first_ref[i] == 1)
    def _cast():
        for w, b in weights:
            _cast_weight(w, b)

    tm, tn = o_ref.shape

    def chunk(r0):
        rows = slice(r0, r0 + MM_ROW_CHUNK)
        x = x_ref[rows, :]
        if mode == "swiglu":
            g = jnp.dot(x, wgb[...], preferred_element_type=F32)
            u = jnp.dot(x, wub[...], preferred_element_type=F32)
            o_ref[rows, :] = (g * (1.0 / (1.0 + jnp.exp(-g))) * u).astype(o_ref.dtype)
            return
        acc = jnp.dot(x, wb[...], preferred_element_type=F32)
        if mode == "residual":
            o_ref[rows, :] = res_ref[rows, :] + acc
        else:
            o_ref[rows, :] = acc.astype(o_ref.dtype)

    @pl.when(nrows_ref[i] == 0)
    def _skip():
        o_ref[...] = jnp.zeros(o_ref.shape, o_ref.dtype)

    half = (tm // (2 * MM_ROW_CHUNK)) * MM_ROW_CHUNK

    @pl.when(nrows_ref[i] > half)
    def _compute():
        for r0 in range(0, tm, MM_ROW_CHUNK):
            chunk(r0)

    @pl.when(jnp.logical_and(nrows_ref[i] > 0, nrows_ref[i] <= half))
    def _compute_half():
        for r0 in range(0, half, MM_ROW_CHUNK):
            chunk(r0)
        o_ref[half:, :] = jnp.zeros((tm - half, tn), o_ref.dtype)


def _dense_plan(n_tiles, tm):
    grp = jnp.zeros((n_tiles,), jnp.int32)
    first = jnp.zeros((n_tiles,), jnp.int32).at[0].set(1)
    nrows = jnp.full((n_tiles,), tm, jnp.int32)
    return grp, first, nrows


def _matmul(x, ws, plan, *, mode, tm, tn, out_dtype, k_blk=None, k_idx=0, extras=()):
    m = x.shape[0]
    _, k, n = ws[0].shape
    kb = k if k_blk is None else k_blk
    grp, first, nrows = plan
    x_spec = pl.BlockSpec((tm, kb), lambda j, i, g, f, v: (i, k_idx))
    w_spec = pl.BlockSpec((1, kb, tn), lambda j, i, g, f, v: (g[i], k_idx, j))
    o_spec = pl.BlockSpec((tm, tn), lambda j, i, g, f, v: (i, j))
    in_specs = [x_spec] + [w_spec] * len(ws)
    if mode == "residual":
        in_specs.append(o_spec)
    grid_spec = pltpu.PrefetchScalarGridSpec(
        num_scalar_prefetch=3,
        grid=(n // tn, m // tm),
        in_specs=in_specs,
        out_specs=o_spec,
        scratch_shapes=[pltpu.VMEM((kb, tn), BF16)] * len(ws),
    )
    return pl.pallas_call(
        functools.partial(_mm_kernel, mode=mode),
        grid_spec=grid_spec,
        out_shape=jax.ShapeDtypeStruct((m, n), out_dtype),
        compiler_params=_params("arbitrary", "arbitrary"),
        name="mm_" + mode,
    )(grp, first, nrows, x, *ws, *extras)


STREAM_CAST_ROWS = 128


def _mm_stream_kernel(x_ref, *rest, mode, qkv_width):
    j = pl.program_id(0)
    i = pl.program_id(1)
    if mode == "swiglu":
        wg_ref, wu_ref, o_ref, wgb, wub = rest
        weights = ((wg_ref, wgb), (wu_ref, wub))
    else:
        w_ref, c_ref, s1_ref, s2_ref, o_ref, wb = rest
        weights = ((w_ref, wb),)
    tm, tn = o_ref.shape
    k_chunk = weights[0][0].shape[1]
    nxt = lax.rem(j, 2)
    cur = 1 - nxt

    def cast_chunk():
        base = i * k_chunk
        for w, b in weights:
            for r0 in range(0, k_chunk, STREAM_CAST_ROWS):
                r = pl.multiple_of(base + r0, STREAM_CAST_ROWS)
                b[nxt, pl.ds(r, STREAM_CAST_ROWS), :] = w[0, r0:r0 + STREAM_CAST_ROWS, :].astype(BF16)

    def compute():
        for r0 in range(0, tm, MM_ROW_CHUNK):
            rows = slice(r0, r0 + MM_ROW_CHUNK)
            x = x_ref[rows, :]
            if mode == "swiglu":
                g = jnp.dot(x, wgb[cur], preferred_element_type=F32)
                u = jnp.dot(x, wub[cur], preferred_element_type=F32)
                o_ref[rows, :] = (g * (1.0 / (1.0 + jnp.exp(-g))) * u).astype(o_ref.dtype)
                continue
            acc = jnp.dot(x, wb[cur], preferred_element_type=F32)
            is_v = ((((j - 1) * tn) // qkv_width) % 3 == 2).astype(jnp.int32)
            keep = jnp.broadcast_to(is_v, (MM_ROW_CHUNK, HEAD_DIM)) > 0
            c, s1, s2 = c_ref[rows, :], s1_ref[rows, :], s2_ref[rows, :]
            for hh in range(tn // HEAD_DIM):
                cs = slice(hh * HEAD_DIM, (hh + 1) * HEAD_DIM)
                a = acc[:, cs]
                o_ref[rows, cs] = jnp.where(keep, a, _rope_rotate(a, c, s1, s2)).astype(o_ref.dtype)

    @pl.when(j == 0)
    def _load_only():
        cast_chunk()
        o_ref[...] = jnp.zeros(o_ref.shape, o_ref.dtype)

    @pl.when(j > 0)
    def _load_and_multiply():
        cast_chunk()
        compute()


def _matmul_stream(x, ws, *, mode, tm, tn, out_dtype, extras=(), qkv_width=0):
    m, k = x.shape
    n = ws[0].shape[2]
    m_tiles = m // tm
    n_tiles = n // tn
    k_chunk = k // m_tiles
    assert k_chunk * m_tiles == k and k_chunk % STREAM_CAST_ROWS == 0
    x_spec = pl.BlockSpec((tm, k), lambda j, i: (i, 0))
    w_spec = pl.BlockSpec((1, k_chunk, tn), lambda j, i: (0, i, jnp.minimum(j, n_tiles - 1)))
    o_spec = pl.BlockSpec((tm, tn), lambda j, i: (i, jnp.where(j == 0, n_tiles, j - 1)))
    in_specs = [x_spec] + [w_spec] * len(ws)
    if mode == "rope":
        in_specs += [pl.BlockSpec((tm, LANES), lambda j, i: (i, 0))] * 3
    return pl.pallas_call(
        functools.partial(_mm_stream_kernel, mode=mode, qkv_width=qkv_width),
        grid=(n_tiles + 1, m_tiles),
        in_specs=in_specs,
        out_specs=o_spec,
        out_shape=jax.ShapeDtypeStruct((m, n + tn), out_dtype),
        scratch_shapes=[pltpu.VMEM((2, k, tn), BF16)] * len(ws),
        compiler_params=_params("arbitrary", "arbitrary"),
        name="mms_" + mode,
    )(x, *ws, *extras)


SPAN = 128
DIL_BLOCK_UNROLL = 32
DIL_VT_UNROLL = 8
MAX_ROW_STRIDE = 8


def _dil_kernel(*refs, seq):
    qkv = refs[:9]
    o_ref = refs[9]
    stage, qp, kp, vt2, og, lg, o_run, l_run = refs[10:]
    n_blocks = seq // SPAN

    key2 = lax.broadcasted_iota(jnp.int32, (2 * SPAN, SPAN), 0)
    qry2 = lax.broadcasted_iota(jnp.int32, (2 * SPAN, SPAN), 1)
    in_cur = jnp.where(key2 >= SPAN, jnp.where(key2 - SPAN <= qry2, 0.0, NEG_INF), NEG_INF)
    bias_first = in_cur
    bias_both = jnp.where(key2 < SPAN, jnp.where(key2 >= qry2, 0.0, NEG_INF), in_cur)

    kp[0:SPAN, :] = jnp.zeros((SPAN, HEAD_DIM), BF16)
    vt2[0, :, 0:SPAN] = jnp.zeros((HEAD_DIM, SPAN), BF16)

    for g, (window, r) in enumerate(DILATED_PAIRS):
        q_ref, k_ref, v_ref = qkv[3 * g:3 * g + 3]
        sub_len = seq // r
        nblk = sub_len // SPAN
        last = g == len(DILATED_PAIRS) - 1

        def regroup(src_ref, dst_ref, off, r=r, sub_len=sub_len):
            if r == 1:
                dst_ref[off:off + seq, :] = src_ref[0].astype(dst_ref.dtype)
                return
            stage[...] = src_ref[0].astype(F32)
            if r <= MAX_ROW_STRIDE:
                for rr in range(r):
                    dst_ref[off + rr * sub_len:off + (rr + 1) * sub_len, :] = (
                        stage[pl.ds(rr, sub_len, stride=r), :].astype(dst_ref.dtype))
                return
            r1 = MAX_ROW_STRIDE // 2
            r2 = r // r1
            assert r % r1 == 0 and r2 <= MAX_ROW_STRIDE
            len1 = seq // r1
            for a in range(r1):
                lg[a * len1:(a + 1) * len1, :] = stage[pl.ds(a, len1, stride=r1), :]
            for rr in range(r):
                a, k = rr % r1, rr // r1
                dst_ref[off + rr * sub_len:off + (rr + 1) * sub_len, :] = (
                    lg[pl.ds(a * len1 + k, sub_len, stride=r2), :].astype(dst_ref.dtype))

        regroup(q_ref, qp, 0)
        regroup(k_ref, kp, SPAN)
        regroup(v_ref, og, 0)

        def vt_body(i, carry):
            base = pl.multiple_of(i * SPAN, SPAN)
            tr = og[pl.ds(base, SPAN), :].T.astype(BF16)
            vt2[i, :, SPAN:2 * SPAN] = tr
            vt2[i + 1, :, 0:SPAN] = tr
            return carry
        lax.fori_loop(0, n_blocks, vt_body, 0, unroll=DIL_VT_UNROLL)

        out_o, out_l = (o_run, l_run) if g == 0 else (og, lg)

        def blk_body(i, carry, nblk=nblk, out_o=out_o, out_l=out_l):
            base = pl.multiple_of(i * SPAN, SPAN)
            q = qp[pl.ds(base, SPAN), :]
            has_prev = jnp.broadcast_to((lax.rem(i, nblk) > 0).astype(jnp.int32), (2 * SPAN, SPAN)) > 0
            s = lax.dot_general(kp[pl.ds(base, 2 * SPAN), :], q, _NT_DIMS,
                                preferred_element_type=F32) * SCALE + jnp.where(has_prev, bias_both, bias_first)
            m = jnp.max(s, axis=0, keepdims=True)
            p = jnp.exp(s - m)
            den = jnp.sum(p, axis=0, keepdims=True)
            o_t = jnp.dot(vt2[i], p.astype(BF16), preferred_element_type=F32)
            o_t = o_t / jnp.maximum(den, TINY)
            lse = m + jnp.log(den)
            out_o[pl.ds(base, SPAN), :] = o_t.T
            out_l[pl.ds(base, SPAN), :] = jnp.broadcast_to(lse, (SPAN, SPAN)).T
            return carry
        lax.fori_loop(0, n_blocks, blk_body, 0, unroll=DIL_BLOCK_UNROLL)

        if g > 0:
            for rr in range(r):
                for c in range(sub_len // SPAN):
                    tok = pl.ds(rr + c * SPAN * r, SPAN, stride=r)
                    rows = pl.ds(rr * sub_len + c * SPAN, SPAN)
                    lp = l_run[tok, :]
                    ln = lg[rows, :]
                    mx = jnp.maximum(lp, ln)
                    wp = jnp.exp(lp - mx)
                    wn = jnp.exp(ln - mx)
                    tot = wp + wn
                    o_run[tok, :] = (o_run[tok, :] * wp + og[rows, :] * wn) / tot
                    if not last:
                        l_run[tok, :] = mx + jnp.log(tot)

    o_ref[0] = o_run[...].astype(o_ref.dtype)


def _dilated_attention(proj, batch, attn_w):
    t, e = proj.shape
    s = t // batch
    n_heads = attn_w // HEAD_DIM
    for window, r in DILATED_PAIRS:
        assert window // r == SPAN and s % (r * SPAN) == 0
    pv = proj.reshape(batch, s, e)

    def spec(g, c):
        return pl.BlockSpec((1, s, HEAD_DIM), lambda b, h: (b, 0, (g * 3 + c) * n_heads + h))

    seq_f32 = pltpu.VMEM((s, HEAD_DIM), F32)
    seq_bf16 = pltpu.VMEM((s, HEAD_DIM), BF16)
    out = pl.pallas_call(
        functools.partial(_dil_kernel, seq=s),
        grid=(batch, n_heads),
        in_specs=[spec(g, c) for g in range(len(DILATED_PAIRS)) for c in range(3)],
        out_specs=pl.BlockSpec((1, s, HEAD_DIM), lambda b, h: (b, 0, h)),
        out_shape=jax.ShapeDtypeStruct((batch, s, attn_w), BF16),
        scratch_shapes=[seq_f32, seq_bf16, pltpu.VMEM((s + SPAN, HEAD_DIM), BF16),
                        pltpu.VMEM((s // SPAN + 1, HEAD_DIM, 2 * SPAN), BF16),
                        seq_f32, seq_f32, seq_f32, seq_f32],
        compiler_params=_params("parallel", "parallel"),
        name="dilated_attn",
    )(*([pv] * 9))
    return out.reshape(t, attn_w)


MOBA_HEADS_PER_STEP = 4
MOBA_BLOCKS_PER_ITER = 4


def _moba_kernel(q_ref, k_ref, v_ref, o_ref, vt, kmean, kparts, sel_ref, acc_ref, *, seq):
    blk = MOBA_BLOCK
    nb = seq // blk
    nbp = kmean.shape[1]
    heads = range(MOBA_HEADS_PER_STEP)
    per_it = MOBA_BLOCKS_PER_ITER
    kmean[...] = jnp.zeros(kmean.shape, F32)
    sel_ref[...] = jnp.full(sel_ref.shape, NEG_INF, F32)

    def head_cols(h):
        return slice(h * HEAD_DIM, (h + 1) * HEAD_DIM)

    def pre(jb, carry):
        r0 = pl.multiple_of(jb * blk, blk)
        for h in heads:
            vt[h, jb] = v_ref[0, pl.ds(r0, blk), head_cols(h)].astype(F32).T.astype(BF16)
            kb = k_ref[0, pl.ds(r0, blk), head_cols(h)].astype(F32)
            kmean[h, pl.ds(jb, 1), :] = jnp.sum(kb, axis=0, keepdims=True) * (1.0 / blk)
        return carry
    lax.fori_loop(0, nb, pre, 0)

    for h in heads:
        km = kmean[h]
        hi = km.astype(BF16)
        r1 = km - hi.astype(F32)
        mid = r1.astype(BF16)
        lo = (r1 - mid.astype(F32)).astype(BF16)
        kparts[h, 0:nbp, :] = hi
        kparts[h, nbp:2 * nbp, :] = mid
        kparts[h, 2 * nbp:3 * nbp, :] = lo

    kblk = lax.broadcasted_iota(jnp.int32, (nbp, blk), 0)
    key = lax.broadcasted_iota(jnp.int32, (blk, blk), 0)
    qry = lax.broadcasted_iota(jnp.int32, (blk, blk), 1)
    bias_causal = jnp.where(key <= qry, 0.0, NEG_INF)

    def q_body(qi, carry):
        r0 = pl.multiple_of(qi * blk, blk)
        qs, ms, ls = [], [], []
        for h in heads:
            q = q_ref[0, pl.ds(r0, blk), head_cols(h)]
            g3 = lax.dot_general(kparts[h], q, _NT_DIMS, preferred_element_type=F32)
            gate = g3[0:nbp] + g3[nbp:2 * nbp] + g3[2 * nbp:3 * nbp]
            gate = jnp.where(kblk < qi, gate, NEG_INF)
            beaten = jnp.zeros((nbp, blk), F32)
            for jp in range(nb):
                gj = gate[jp:jp + 1, :]
                tie = jnp.where(kblk > jp, 1.0, 0.0)
                beaten = beaten + jnp.where(gj > gate, 1.0, jnp.where(gj == gate, tie, 0.0))
            sel_ref[h, 0:nbp, :] = jnp.where(beaten < MOBA_TOPK, jnp.where(kblk < qi, 0.0, NEG_INF), NEG_INF)

            s = lax.dot_general(k_ref[0, pl.ds(r0, blk), head_cols(h)], q, _NT_DIMS,
                                preferred_element_type=F32) * SCALE + bias_causal
            m = jnp.max(s, axis=0, keepdims=True)
            p = jnp.exp(s - m)
            acc_ref[h] = jnp.dot(vt[h, qi], p.astype(BF16), preferred_element_type=F32)
            qs.append(q)
            ms.append(m)
            ls.append(jnp.sum(p, axis=0, keepdims=True))

        def past(it, st):
            ms, ls = st
            new_m, new_l = [], []
            for h in heads:
                m, l = ms[h], ls[h]
                blocks = [it * per_it + u for u in range(per_it)]
                ss = []
                for jb in blocks:
                    c0 = pl.multiple_of(jnp.minimum(jb, nb - 1) * blk, blk)
                    ss.append(lax.dot_general(k_ref[0, pl.ds(c0, blk), head_cols(h)], qs[h], _NT_DIMS,
                                              preferred_element_type=F32) * SCALE + sel_ref[h, pl.ds(jb, 1), :])
                m_new = m
                for s in ss:
                    m_new = jnp.maximum(m_new, jnp.max(s, axis=0, keepdims=True))
                alpha = jnp.exp(m - m_new)
                l = alpha * l
                acc = alpha * acc_ref[h]
                for jb, s in zip(blocks, ss):
                    p = jnp.exp(s - m_new)
                    l = l + jnp.sum(p, axis=0, keepdims=True)
                    acc = acc + jnp.dot(vt[h, jnp.minimum(jb, nb - 1)], p.astype(BF16),
                                        preferred_element_type=F32)
                acc_ref[h] = acc
                new_m.append(m_new)
                new_l.append(l)
            return tuple(new_m), tuple(new_l)

        ms, ls = lax.fori_loop(0, (qi + per_it - 1) // per_it, past, (tuple(ms), tuple(ls)))
        for h in heads:
            o_ref[0, pl.ds(r0, blk), head_cols(h)] = (
                (acc_ref[h] / jnp.maximum(ls[h], TINY)).T.astype(o_ref.dtype))
        return carry

    lax.fori_loop(0, nb, q_body, 0)


def _moba_attention(proj, batch, attn_w):
    t, e = proj.shape
    s = t // batch
    assert s % MOBA_BLOCK == 0
    nb = s // MOBA_BLOCK
    nbp = -(-nb // BF16_ROWS) * BF16_ROWS
    n_heads = attn_w // HEAD_DIM
    hp = MOBA_HEADS_PER_STEP
    assert n_heads % hp == 0
    w = hp * HEAD_DIM
    pv = proj.reshape(batch, s, e)

    def spec(c):
        return pl.BlockSpec((1, s, w), lambda b, h: (b, 0, c * (n_heads // hp) + h))

    out = pl.pallas_call(
        functools.partial(_moba_kernel, seq=s),
        grid=(batch, n_heads // hp),
        in_specs=[spec(0), spec(1), spec(2)],
        out_specs=pl.BlockSpec((1, s, w), lambda b, h: (b, 0, h)),
        out_shape=jax.ShapeDtypeStruct((batch, s, attn_w), BF16),
        scratch_shapes=[pltpu.VMEM((hp, nb, HEAD_DIM, MOBA_BLOCK), BF16),
                        pltpu.VMEM((hp, nbp, HEAD_DIM), F32),
                        pltpu.VMEM((hp, 3 * nbp, HEAD_DIM), BF16),
                        pltpu.VMEM((hp, nbp + MOBA_BLOCKS_PER_ITER, MOBA_BLOCK), F32),
                        pltpu.VMEM((hp, HEAD_DIM, MOBA_BLOCK), F32)],
        compiler_params=_params("parallel", "parallel"),
        name="moba_attn",
    )(pv, pv, pv)
    return out.reshape(t, attn_w)


def _router_kernel(x_ref, g_ref, wr_ref, i1_ref, i2_ref, g1_ref, g2_ref, *, n_experts):
    y = _rms(x_ref[...], g_ref[...])
    lg = jnp.dot(y, wr_ref[...], preferred_element_type=F32, precision=lax.Precision.HIGHEST)
    lane = lax.broadcasted_iota(jnp.int32, lg.shape, 1).astype(F32)
    lg = jnp.where(lane < n_experts, lg, NEG_INF)
    m1 = jnp.max(lg, axis=1, keepdims=True)
    i1 = jnp.min(jnp.where(lg == m1, lane, float(LANES)), axis=1, keepdims=True)
    lg2 = jnp.where(lane == i1, NEG_INF, lg)
    m2 = jnp.max(lg2, axis=1, keepdims=True)
    i2 = jnp.min(jnp.where(lg2 == m2, lane, float(LANES)), axis=1, keepdims=True)
    e = jnp.exp(m2 - m1)
    i1_ref[...] = i1.astype(jnp.int32)
    i2_ref[...] = i2.astype(jnp.int32)
    g1_ref[...] = 1.0 / (1.0 + e)
    g2_ref[...] = e / (1.0 + e)


def _router(h, g, w_router):
    t, d = h.shape
    n_experts = w_router.shape[1]
    assert n_experts <= LANES
    tm = 256
    wr = jnp.zeros((d, LANES), F32).at[:, :n_experts].set(w_router)
    col = pl.BlockSpec((tm, 1), lambda i: (i, 0))
    return pl.pallas_call(
        functools.partial(_router_kernel, n_experts=n_experts),
        grid=(t // tm,),
        in_specs=[pl.BlockSpec((tm, d), lambda i: (i, 0)), pl.BlockSpec((1, d), lambda i: (0, 0)),
                  pl.BlockSpec((d, LANES), lambda i: (0, 0))],
        out_specs=[col, col, col, col],
        out_shape=[jax.ShapeDtypeStruct((t, 1), jnp.int32)] * 2 + [jax.ShapeDtypeStruct((t, 1), F32)] * 2,
        compiler_params=_params("parallel"),
        name="router",
    )(h, g.reshape(1, d), wr)


def _route_plan(i1, i2, n_experts, tm):
    t = i1.shape[0]
    e = jnp.concatenate([i1, i2])
    onehot = (e[:, None] == jnp.arange(n_experts, dtype=jnp.int32)[None, :]).astype(jnp.int32)
    csum = jnp.cumsum(onehot, axis=0)
    rank = jnp.take_along_axis(csum, e[:, None], axis=1)[:, 0] - 1
    counts = csum[-1]
    padded = ((counts + tm - 1) // tm) * tm
    ends = jnp.cumsum(padded)
    starts = ends - padded
    pos = starts[e] + rank
    n_rows = TOP_K * t + n_experts * tm
    tok = jnp.arange(t, dtype=jnp.int32)
    row_token = jnp.zeros((n_rows,), jnp.int32).at[pos].set(jnp.concatenate([tok, tok]))
    tile_start = jnp.arange(n_rows // tm, dtype=jnp.int32) * tm
    grp = jnp.sum((tile_start[:, None] >= ends[None, :]).astype(jnp.int32), axis=1)
    in_use = grp < n_experts
    grp = jnp.minimum(grp, n_experts - 1)
    nrows = jnp.where(in_use, jnp.clip((starts + counts)[grp] - tile_start, 0, tm), 0).astype(jnp.int32)
    first = jnp.concatenate([jnp.ones((1,), jnp.int32), (grp[1:] != grp[:-1]).astype(jnp.int32)])
    g_start = jnp.arange(n_rows // GATHER_ROWS, dtype=jnp.int32) * GATHER_ROWS
    g_valid = ((g_start % tm) < nrows[g_start // tm]).astype(jnp.int32)
    return (grp, first, nrows), row_token, g_valid, pos[:t], pos[t:]


GATHER_ROWS = 256
DMA_ISSUE_UNROLL = 8


def _gather_norm_kernel(tok_ref, valid_ref, g_ref, h_hbm, o_ref, buf, sem, *, tm):
    i = pl.program_id(0)

    def row_copy(r):
        return pltpu.make_async_copy(h_hbm.at[pl.ds(tok_ref[i * tm + r], 1)], buf.at[pl.ds(r, 1)], sem)

    @pl.when(valid_ref[i] == 0)
    def _skip():
        o_ref[...] = jnp.zeros(o_ref.shape, o_ref.dtype)

    @pl.when(valid_ref[i] == 1)
    def _gather():
        def start(r, carry):
            row_copy(r).start()
            return carry
        lax.fori_loop(0, tm, start, 0, unroll=DMA_ISSUE_UNROLL)

        def wait(r, carry):
            row_copy(r).wait()
            return carry
        lax.fori_loop(0, tm, wait, 0, unroll=DMA_ISSUE_UNROLL)
        o_ref[...] = _rms(buf[...], g_ref[...]).astype(o_ref.dtype)


def _gather_norm(h, g, row_token, valid, tm):
    t, d = h.shape
    n_rows = row_token.shape[0]
    grid_spec = pltpu.PrefetchScalarGridSpec(
        num_scalar_prefetch=2,
        grid=(n_rows // tm,),
        in_specs=[pl.BlockSpec((1, d), lambda i, tok, v: (0, 0)), pl.BlockSpec(memory_space=pl.ANY)],
        out_specs=pl.BlockSpec((tm, d), lambda i, tok, v: (i, 0)),
        scratch_shapes=[pltpu.VMEM((tm, d), F32), pltpu.SemaphoreType.DMA(())],
    )
    return pl.pallas_call(
        functools.partial(_gather_norm_kernel, tm=tm),
        grid_spec=grid_spec,
        out_shape=jax.ShapeDtypeStruct((n_rows, d), BF16),
        compiler_params=_params("arbitrary"),
        name="moe_gather_norm",
    )(row_token, valid, g.reshape(1, d), h)


def _combine_kernel(p1_ref, p2_ref, h_ref, g1_ref, g2_ref, fn_ref, y_hbm, o_ref, buf, sem, *, tm):
    i = pl.program_id(0)

    def row_copy(r, which, p_ref):
        return pltpu.make_async_copy(y_hbm.at[pl.ds(p_ref[i * tm + r], 1)], buf.at[which, pl.ds(r, 1)], sem)

    def start(r, carry):
        row_copy(r, 0, p1_ref).start()
        row_copy(r, 1, p2_ref).start()
        return carry
    lax.fori_loop(0, tm, start, 0, unroll=DMA_ISSUE_UNROLL)

    def wait(r, carry):
        row_copy(r, 0, p1_ref).wait()
        row_copy(r, 1, p2_ref).wait()
        return carry
    lax.fori_loop(0, tm, wait, 0, unroll=DMA_ISSUE_UNROLL)
    h = h_ref[...] + (g1_ref[...] * buf[0] + g2_ref[...] * buf[1])
    o_ref[...] = _rms(h, fn_ref[...])


def _combine_norm(h, y, pos1, pos2, g1, g2, final_norm):
    t, d = h.shape
    tm = 256
    col = pl.BlockSpec((tm, 1), lambda i, a, b: (i, 0))
    row = pl.BlockSpec((tm, d), lambda i, a, b: (i, 0))
    grid_spec = pltpu.PrefetchScalarGridSpec(
        num_scalar_prefetch=2,
        grid=(t // tm,),
        in_specs=[row, col, col, pl.BlockSpec((1, d), lambda i, a, b: (0, 0)),
                  pl.BlockSpec(memory_space=pl.ANY)],
        out_specs=row,
        scratch_shapes=[pltpu.VMEM((TOP_K, tm, d), F32), pltpu.SemaphoreType.DMA(())],
    )
    return pl.pallas_call(
        functools.partial(_combine_kernel, tm=tm),
        grid_spec=grid_spec,
        out_shape=jax.ShapeDtypeStruct((t, d), F32),
        compiler_params=_params("arbitrary"),
        name="moe_combine_norm",
    )(pos1, pos2, h, g1, g2, final_norm.reshape(1, d), y)


TM_DENSE = 1024
TN_DENSE = 512
TN_STREAM_ROPE = 1024
TN_STREAM_SWIGLU = 512
TM_EXPERT = 512
TN_SWIGLU_EXPERT = 512
TN_DOWN_EXPERT = 1024
DOWN_K_SPLITS = 4


def _down_proj(a, w, plan, res, tm, tn):
    k = w.shape[1]
    splits = DOWN_K_SPLITS if k > 8192 else 1
    kb = k // splits
    for ki in range(splits):
        res = _matmul(a, (w,), plan, mode="residual", tm=tm, tn=tn, out_dtype=F32,
                      k_blk=kb, k_idx=ki, extras=(res,))
    return res


def kernel(x, positions, mix_norm, ffn_norm, dil_w_in, dil_w_out, moba_w_in, moba_w_out,
           ffn_w_gate, ffn_w_up, ffn_w_down, router_w, exp_w_gate, exp_w_up, exp_w_down,
           final_norm):
    batch, s, d = x.shape
    t = batch * s
    attn_w = dil_w_out.shape[1]
    h = x.reshape(t, d)
    rope = _rope_tables(positions)
    dense = _dense_plan(t // TM_DENSE, TM_DENSE)

    hn = _rmsnorm(h, mix_norm[0])
    proj = _matmul_stream(hn, (dil_w_in,), mode="rope", tm=TM_DENSE, tn=TN_STREAM_ROPE, out_dtype=BF16,
                          extras=rope, qkv_width=attn_w)
    o = _dilated_attention(proj, batch, attn_w)
    h = _matmul(o, (dil_w_out,), dense, mode="residual", tm=TM_DENSE, tn=TN_DENSE, out_dtype=F32,
                extras=(h,))
    hn = _rmsnorm(h, ffn_norm[0])
    a = _matmul_stream(hn, (ffn_w_gate, ffn_w_up), mode="swiglu", tm=TM_DENSE, tn=TN_STREAM_SWIGLU,
                       out_dtype=BF16)
    h = _down_proj(a, ffn_w_down, dense, h, TM_DENSE, TN_DENSE)

    hn = _rmsnorm(h, mix_norm[1])
    proj = _matmul_stream(hn, (moba_w_in,), mode="rope", tm=TM_DENSE, tn=TN_STREAM_ROPE, out_dtype=BF16,
                          extras=rope, qkv_width=attn_w)
    o = _moba_attention(proj, batch, attn_w)
    h = _matmul(o, (moba_w_out,), dense, mode="residual", tm=TM_DENSE, tn=TN_DENSE, out_dtype=F32,
                extras=(h,))

    n_experts = router_w.shape[-1]
    i1, i2, g1, g2 = _router(h, ffn_norm[1], router_w[0])
    plan, row_token, g_valid, pos1, pos2 = _route_plan(i1[:, 0], i2[:, 0], n_experts, TM_EXPERT)
    xs = _gather_norm(h, ffn_norm[1], row_token, g_valid, GATHER_ROWS)
    a = _matmul(xs, (exp_w_gate[0], exp_w_up[0]), plan, mode="swiglu", tm=TM_EXPERT,
                tn=TN_SWIGLU_EXPERT, out_dtype=BF16)
    y = _matmul(a, (exp_w_down[0],), plan, mode="plain", tm=TM_EXPERT, tn=TN_DOWN_EXPERT, out_dtype=F32)
    out = _combine_norm(h, y, pos1, pos2, g1, g2, final_norm)
    return out.reshape(batch, s, d)
```

```python
import functools

import jax
import jax.numpy as jnp
from jax import lax
from jax.experimental import pallas as pl
from jax.experimental.pallas import tpu as pltpu

HEAD_DIM = 128
ROPE_DIM = HEAD_DIM // 4
ROPE_HALF = ROPE_DIM // 2
ROPE_THETA = 500000.0
SCALE = HEAD_DIM ** -0.5
NORM_EPS = 1e-5
TINY = 1e-30
DILATED_PAIRS = ((128, 1), (512, 4), (2048, 16))
MOBA_BLOCK = 256
MOBA_TOPK = 3
TOP_K = 2

LANES = 128
BF16_ROWS = 16
VMEM_LIMIT_BYTES = 56 * 2 ** 20

NEG_INF = float("-inf")
F32 = jnp.float32
BF16 = jnp.bfloat16

_NT_DIMS = (((1,), (1,)), ((), ()))


def _params(*sem):
    return pltpu.CompilerParams(dimension_semantics=sem, vmem_limit_bytes=VMEM_LIMIT_BYTES)


def _rope_table_kernel(pos_ref, invf_ref, c_ref, s1_ref, s2_ref):
    ang = pos_ref[...] * invf_ref[...]
    lane = lax.broadcasted_iota(jnp.int32, ang.shape, 1)
    c = jnp.cos(ang)
    s = jnp.sin(ang)
    c_ref[...] = jnp.where(lane < ROPE_DIM, c, 1.0)
    s1_ref[...] = jnp.where(lane < ROPE_HALF, -s, 0.0)
    s2_ref[...] = jnp.where(lane < ROPE_HALF, 0.0, jnp.where(lane < ROPE_DIM, s, 0.0))


def _rope_tables(positions):
    t = positions.size
    tm = 1024
    pos = positions.reshape(t, 1).astype(F32)
    inv_freq = jnp.power(ROPE_THETA, -jnp.arange(ROPE_HALF, dtype=F32) / ROPE_HALF)
    invf = jnp.zeros((1, LANES), F32).at[0, :ROPE_DIM].set(jnp.tile(inv_freq, 2))
    tab = jax.ShapeDtypeStruct((t, LANES), F32)
    spec = pl.BlockSpec((tm, LANES), lambda i: (i, 0))
    return pl.pallas_call(
        _rope_table_kernel,
        grid=(t // tm,),
        in_specs=[pl.BlockSpec((tm, 1), lambda i: (i, 0)), pl.BlockSpec((1, LANES), lambda i: (0, 0))],
        out_specs=[spec, spec, spec],
        out_shape=[tab, tab, tab],
        compiler_params=_params("parallel"),
        name="rope_tables",
    )(pos, invf)


def _rms(x, g):
    ms = jnp.mean(x * x, axis=-1, keepdims=True)
    return x * lax.rsqrt(ms + NORM_EPS) * g


def _rmsnorm_kernel(x_ref, g_ref, o_ref):
    o_ref[...] = _rms(x_ref[...], g_ref[...]).astype(o_ref.dtype)


def _rmsnorm(x, g, out_dtype=BF16):
    t, d = x.shape
    tm = 512
    return pl.pallas_call(
        _rmsnorm_kernel,
        grid=(t // tm,),
        in_specs=[pl.BlockSpec((tm, d), lambda i: (i, 0)), pl.BlockSpec((1, d), lambda i: (0, 0))],
        out_specs=pl.BlockSpec((tm, d), lambda i: (i, 0)),
        out_shape=jax.ShapeDtypeStruct((t, d), out_dtype),
        compiler_params=_params("parallel"),
        name="rmsnorm",
    )(x, g.reshape(1, d))


CAST_ROWS = 256
MM_ROW_CHUNK = 256


def _cast_weight(w_ref, wb_ref):
    def body(c, carry):
        r = pl.multiple_of(c * CAST_ROWS, CAST_ROWS)
        wb_ref[pl.ds(r, CAST_ROWS), :] = w_ref[0, pl.ds(r, CAST_ROWS), :].astype(wb_ref.dtype)
        return carry
    lax.fori_loop(0, wb_ref.shape[0] // CAST_ROWS, body, 0)


def _rope_rotate(acc, c, s1, s2):
    return (acc * c + pltpu.roll(acc, LANES - ROPE_HALF, 1) * s1
            + pltpu.roll(acc, ROPE_HALF, 1) * s2)


def _mm_kernel(grp_ref, first_ref, nrows_ref, x_ref, *rest, mode):
    del grp_ref
    i = pl.program_id(1)
    if mode == "swiglu":
        wg_ref, wu_ref, o_ref, wgb, wub = rest
        weights = ((wg_ref, wgb), (wu_ref, wub))
    elif mode == "residual":
        w_ref, res_ref, o_ref, wb = rest
        weights = ((w_ref, wb),)
    else:
        w_ref, o_ref, wb = rest
        weights = ((w_ref, wb),)

    @pl.when(first_ref[i] == 1)
    def _cast():
        for w, b in weights:
            _cast_weight(w, b)

    tm, tn = o_ref.shape

    def chunk(r0):
        rows = slice(r0, r0 + MM_ROW_CHUNK)
        x = x_ref[rows, :]
        if mode == "swiglu":
            g = jnp.dot(x, wgb[...], preferred_element_type=F32)
            u = jnp.dot(x, wub[...], preferred_element_type=F32)
            o_ref[rows, :] = (g * (1.0 / (1.0 + jnp.exp(-g))) * u).astype(o_ref.dtype)
            return
        acc = jnp.dot(x, wb[...], preferred_element_type=F32)
        if mode == "residual":
            o_ref[rows, :] = res_ref[rows, :] + acc
        else:
            o_ref[rows, :] = acc.astype(o_ref.dtype)

    @pl.when(nrows_ref[i] == 0)
    def _skip():
        o_ref[...] = jnp.zeros(o_ref.shape, o_ref.dtype)

    half = (tm // (2 * MM_ROW_CHUNK)) * MM_ROW_CHUNK

    @pl.when(nrows_ref[i] > half)
    def _compute():
        for r0 in range(0, tm, MM_ROW_CHUNK):
            chunk(r0)

    @pl.when(jnp.logical_and(nrows_ref[i] > 0, nrows_ref[i] <= half))
    def _compute_half():
        for r0 in range(0, half, MM_ROW_CHUNK):
            chunk(r0)
        o_ref[half:, :] = jnp.zeros((tm - half, tn), o_ref.dtype)


def _dense_plan(n_tiles, tm):
    grp = jnp.zeros((n_tiles,), jnp.int32)
    first = jnp.zeros((n_tiles,), jnp.int32).at[0].set(1)
    nrows = jnp.full((n_tiles,), tm, jnp.int32)
    return grp, first, nrows


def _matmul(x, ws, plan, *, mode, tm, tn, out_dtype, k_blk=None, k_idx=0, extras=()):
    m = x.shape[0]
    _, k, n = ws[0].shape
    kb = k if k_blk is None else k_blk
    grp, first, nrows = plan
    x_spec = pl.BlockSpec((tm, kb), lambda j, i, g, f, v: (i, k_idx))
    w_spec = pl.BlockSpec((1, kb, tn), lambda j, i, g, f, v: (g[i], k_idx, j))
    o_spec = pl.BlockSpec((tm, tn), lambda j, i, g, f, v: (i, j))
    in_specs = [x_spec] + [w_spec] * len(ws)
    if mode == "residual":
        in_specs.append(o_spec)
    grid_spec = pltpu.PrefetchScalarGridSpec(
        num_scalar_prefetch=3,
        grid=(n // tn, m // tm),
        in_specs=in_specs,
        out_specs=o_spec,
        scratch_shapes=[pltpu.VMEM((kb, tn), BF16)] * len(ws),
    )
    return pl.pallas_call(
        functools.partial(_mm_kernel, mode=mode),
        grid_spec=grid_spec,
        out_shape=jax.ShapeDtypeStruct((m, n), out_dtype),
        compiler_params=_params("arbitrary", "arbitrary"),
        name="mm_" + mode,
    )(grp, first, nrows, x, *ws, *extras)


STREAM_CAST_ROWS = 128


def _mm_stream_kernel(x_ref, *rest, mode, qkv_width):
    j = pl.program_id(0)
    i = pl.program_id(1)
    if mode == "swiglu":
        wg_ref, wu_ref, o_ref, wgb, wub = rest
        weights = ((wg_ref, wgb), (wu_ref, wub))
    else:
        w_ref, c_ref, s1_ref, s2_ref, o_ref, wb = rest
        weights = ((w_ref, wb),)
    tm, tn = o_ref.shape
    k_chunk = weights[0][0].shape[1]
    nxt = lax.rem(j, 2)
    cur = 1 - nxt

    def cast_chunk():
        base = i * k_chunk
        for w, b in weights:
            for r0 in range(0, k_chunk, STREAM_CAST_ROWS):
                r = pl.multiple_of(base + r0, STREAM_CAST_ROWS)
                b[nxt, pl.ds(r, STREAM_CAST_ROWS), :] = w[0, r0:r0 + STREAM_CAST_ROWS, :].astype(BF16)

    def compute():
        for r0 in range(0, tm, MM_ROW_CHUNK):
            rows = slice(r0, r0 + MM_ROW_CHUNK)
            x = x_ref[rows, :]
            if mode == "swiglu":
                g = jnp.dot(x, wgb[cur], preferred_element_type=F32)
                u = jnp.dot(x, wub[cur], preferred_element_type=F32)
                o_ref[rows, :] = (g * (1.0 / (1.0 + jnp.exp(-g))) * u).astype(o_ref.dtype)
                continue
            acc = jnp.dot(x, wb[cur], preferred_element_type=F32)
            is_v = ((((j - 1) * tn) // qkv_width) % 3 == 2).astype(jnp.int32)
            keep = jnp.broadcast_to(is_v, (MM_ROW_CHUNK, HEAD_DIM)) > 0
            c, s1, s2 = c_ref[rows, :], s1_ref[rows, :], s2_ref[rows, :]
            for hh in range(tn // HEAD_DIM):
                cs = slice(hh * HEAD_DIM, (hh + 1) * HEAD_DIM)
                a = acc[:, cs]
                o_ref[rows, cs] = jnp.where(keep, a, _rope_rotate(a, c, s1, s2)).astype(o_ref.dtype)

    @pl.when(j == 0)
    def _load_only():
        cast_chunk()
        o_ref[...] = jnp.zeros(o_ref.shape, o_ref.dtype)

    @pl.when(j > 0)
    def _load_and_multiply():
        cast_chunk()
        compute()


def _matmul_stream(x, ws, *, mode, tm, tn, out_dtype, extras=(), qkv_width=0):
    m, k = x.shape
    n = ws[0].shape[2]
    m_tiles = m // tm
    n_tiles = n // tn
    k_chunk = k // m_tiles
    assert k_chunk * m_tiles == k and k_chunk % STREAM_CAST_ROWS == 0
    x_spec = pl.BlockSpec((tm, k), lambda j, i: (i, 0))
    w_spec = pl.BlockSpec((1, k_chunk, tn), lambda j, i: (0, i, jnp.minimum(j, n_tiles - 1)))
    o_spec = pl.BlockSpec((tm, tn), lambda j, i: (i, jnp.where(j == 0, n_tiles, j - 1)))
    in_specs = [x_spec] + [w_spec] * len(ws)
    if mode == "rope":
        in_specs += [pl.BlockSpec((tm, LANES), lambda j, i: (i, 0))] * 3
    return pl.pallas_call(
        functools.partial(_mm_stream_kernel, mode=mode, qkv_width=qkv_width),
        grid=(n_tiles + 1, m_tiles),
        in_specs=in_specs,
        out_specs=o_spec,
        out_shape=jax.ShapeDtypeStruct((m, n + tn), out_dtype),
        scratch_shapes=[pltpu.VMEM((2, k, tn), BF16)] * len(ws),
        compiler_params=_params("arbitrary", "arbitrary"),
        name="mms_" + mode,
    )(x, *ws, *extras)


SPAN = 128
DIL_BLOCK_UNROLL = 32
DIL_VT_UNROLL = 8
MAX_ROW_STRIDE = 8


def _dil_kernel(*refs, seq):
    qkv = refs[:9]
    o_ref = refs[9]
    stage, qp, kp, vt2, og, lg, o_run, l_run = refs[10:]
    n_blocks = seq // SPAN

    key2 = lax.broadcasted_iota(jnp.int32, (2 * SPAN, SPAN), 0)
    qry2 = lax.broadcasted_iota(jnp.int32, (2 * SPAN, SPAN), 1)
    in_cur = jnp.where(key2 >= SPAN, jnp.where(key2 - SPAN <= qry2, 0.0, NEG_INF), NEG_INF)
    bias_first = in_cur
    bias_both = jnp.where(key2 < SPAN, jnp.where(key2 >= qry2, 0.0, NEG_INF), in_cur)

    kp[0:SPAN, :] = jnp.zeros((SPAN, HEAD_DIM), BF16)
    vt2[0, :, 0:SPAN] = jnp.zeros((HEAD_DIM, SPAN), BF16)

    for g, (window, r) in enumerate(DILATED_PAIRS):
        q_ref, k_ref, v_ref = qkv[3 * g:3 * g + 3]
        sub_len = seq // r
        nblk = sub_len // SPAN
        last = g == len(DILATED_PAIRS) - 1

        def regroup(src_ref, dst_ref, off, r=r, sub_len=sub_len):
            if r == 1:
                dst_ref[off:off + seq, :] = src_ref[0].astype(dst_ref.dtype)
                return
            stage[...] = src_ref[0].astype(F32)
            if r <= MAX_ROW_STRIDE:
                for rr in range(r):
                    dst_ref[off + rr * sub_len:off + (rr + 1) * sub_len, :] = (
                        stage[pl.ds(rr, sub_len, stride=r), :].astype(dst_ref.dtype))
                return
            r1 = MAX_ROW_STRIDE // 2
            r2 = r // r1
            assert r % r1 == 0 and r2 <= MAX_ROW_STRIDE
            len1 = seq // r1
            for a in range(r1):
                lg[a * len1:(a + 1) * len1, :] = stage[pl.ds(a, len1, stride=r1), :]
            for rr in range(r):
                a, k = rr % r1, rr // r1
                dst_ref[off + rr * sub_len:off + (rr + 1) * sub_len, :] = (
                    lg[pl.ds(a * len1 + k, sub_len, stride=r2), :].astype(dst_ref.dtype))

        regroup(q_ref, qp, 0)
        regroup(k_ref, kp, SPAN)
        regroup(v_ref, og, 0)

        def vt_body(i, carry):
            base = pl.multiple_of(i * SPAN, SPAN)
            tr = og[pl.ds(base, SPAN), :].T.astype(BF16)
            vt2[i, :, SPAN:2 * SPAN] = tr
            vt2[i + 1, :, 0:SPAN] = tr
            return carry
        lax.fori_loop(0, n_blocks, vt_body, 0, unroll=DIL_VT_UNROLL)

        out_o, out_l = (o_run, l_run) if g == 0 else (og, lg)

        def blk_body(i, carry, nblk=nblk, out_o=out_o, out_l=out_l):
            base = pl.multiple_of(i * SPAN, SPAN)
            q = qp[pl.ds(base, SPAN), :]
            has_prev = jnp.broadcast_to((lax.rem(i, nblk) > 0).astype(jnp.int32), (2 * SPAN, SPAN)) > 0
            s = lax.dot_general(kp[pl.ds(base, 2 * SPAN), :], q, _NT_DIMS,
                                preferred_element_type=F32) * SCALE + jnp.where(has_prev, bias_both, bias_first)
            m = jnp.max(s, axis=0, keepdims=True)
            p = jnp.exp(s - m)
            den = jnp.sum(p, axis=0, keepdims=True)
            o_t = jnp.dot(vt2[i], p.astype(BF16), preferred_element_type=F32)
            o_t = o_t / jnp.maximum(den, TINY)
            lse = m + jnp.log(den)
            out_o[pl.ds(base, SPAN), :] = o_t.T
            out_l[pl.ds(base, SPAN), :] = jnp.broadcast_to(lse, (SPAN, SPAN)).T
            return carry
        lax.fori_loop(0, n_blocks, blk_body, 0, unroll=DIL_BLOCK_UNROLL)

        if g > 0:
            for rr in range(r):
                for c in range(sub_len // SPAN):
                    tok = pl.ds(rr + c * SPAN * r, SPAN, stride=r)
                    rows = pl.ds(rr * sub_len + c * SPAN, SPAN)
                    lp = l_run[tok, :]
                    ln = lg[rows, :]
                    mx = jnp.maximum(lp, ln)
                    wp = jnp.exp(lp - mx)
                    wn = jnp.exp(ln - mx)
                    tot = wp + wn
                    o_run[tok, :] = (o_run[tok, :] * wp + og[rows, :] * wn) / tot
                    if not last:
                        l_run[tok, :] = mx + jnp.log(tot)

    o_ref[0] = o_run[...].astype(o_ref.dtype)


def _dilated_attention(proj, batch, attn_w):
    t, e = proj.shape
    s = t // batch
    n_heads = attn_w // HEAD_DIM
    for window, r in DILATED_PAIRS:
        assert window // r == SPAN and s % (r * SPAN) == 0
    pv = proj.reshape(batch, s, e)

    def spec(g, c):
        return pl.BlockSpec((1, s, HEAD_DIM), lambda b, h: (b, 0, (g * 3 + c) * n_heads + h))

    seq_f32 = pltpu.VMEM((s, HEAD_DIM), F32)
    seq_bf16 = pltpu.VMEM((s, HEAD_DIM), BF16)
    out = pl.pallas_call(
        functools.partial(_dil_kernel, seq=s),
        grid=(batch, n_heads),
        in_specs=[spec(g, c) for g in range(len(DILATED_PAIRS)) for c in range(3)],
        out_specs=pl.BlockSpec((1, s, HEAD_DIM), lambda b, h: (b, 0, h)),
        out_shape=jax.ShapeDtypeStruct((batch, s, attn_w), BF16),
        scratch_shapes=[seq_f32, seq_bf16, pltpu.VMEM((s + SPAN, HEAD_DIM), BF16),
                        pltpu.VMEM((s // SPAN + 1, HEAD_DIM, 2 * SPAN), BF16),
                        seq_f32, seq_f32, seq_f32, seq_f32],
        compiler_params=_params("parallel", "parallel"),
        name="dilated_attn",
    )(*([pv] * 9))
    return out.reshape(t, attn_w)


MOBA_HEADS_PER_STEP = 4
MOBA_BLOCKS_PER_ITER = 4


def _moba_kernel(q_ref, k_ref, v_ref, o_ref, vt, kmean, kparts, sel_ref, acc_ref, *, seq):
    blk = MOBA_BLOCK
    nb = seq // blk
    nbp = kmean.shape[1]
    heads = range(MOBA_HEADS_PER_STEP)
    per_it = MOBA_BLOCKS_PER_ITER
    kmean[...] = jnp.zeros(kmean.shape, F32)
    sel_ref[...] = jnp.full(sel_ref.shape, NEG_INF, F32)

    def head_cols(h):
        return slice(h * HEAD_DIM, (h + 1) * HEAD_DIM)

    def pre(jb, carry):
        r0 = pl.multiple_of(jb * blk, blk)
        for h in heads:
            vt[h, jb] = v_ref[0, pl.ds(r0, blk), head_cols(h)].astype(F32).T.astype(BF16)
            kb = k_ref[0, pl.ds(r0, blk), head_cols(h)].astype(F32)
            kmean[h, pl.ds(jb, 1), :] = jnp.sum(kb, axis=0, keepdims=True) * (1.0 / blk)
        return carry
    lax.fori_loop(0, nb, pre, 0)

    for h in heads:
        km = kmean[h]
        hi = km.astype(BF16)
        r1 = km - hi.astype(F32)
        mid = r1.astype(BF16)
        lo = (r1 - mid.astype(F32)).astype(BF16)
        kparts[h, 0:nbp, :] = hi
        kparts[h, nbp:2 * nbp, :] = mid
        kparts[h, 2 * nbp:3 * nbp, :] = lo

    kblk = lax.broadcasted_iota(jnp.int32, (nbp, blk), 0)
    key = lax.broadcasted_iota(jnp.int32, (blk, blk), 0)
    qry = lax.broadcasted_iota(jnp.int32, (blk, blk), 1)
    bias_causal = jnp.where(key <= qry, 0.0, NEG_INF)

    def q_body(qi, carry):
        r0 = pl.multiple_of(qi * blk, blk)
        qs, ms, ls = [], [], []
        for h in heads:
            q = q_ref[0, pl.ds(r0, blk), head_cols(h)]
            g3 = lax.dot_general(kparts[h], q, _NT_DIMS, preferred_element_type=F32)
            gate = g3[0:nbp] + g3[nbp:2 * nbp] + g3[2 * nbp:3 * nbp]
            gate = jnp.where(kblk < qi, gate, NEG_INF)
            beaten = jnp.zeros((nbp, blk), F32)
            for jp in range(nb):
                gj = gate[jp:jp + 1, :]
                tie = jnp.where(kblk > jp, 1.0, 0.0)
                beaten = beaten + jnp.where(gj > gate, 1.0, jnp.where(gj == gate, tie, 0.0))
            sel_ref[h, 0:nbp, :] = jnp.where(beaten < MOBA_TOPK, jnp.where(kblk < qi, 0.0, NEG_INF), NEG_INF)

            s = lax.dot_general(k_ref[0, pl.ds(r0, blk), head_cols(h)], q, _NT_DIMS,
                                preferred_element_type=F32) * SCALE + bias_causal
            m = jnp.max(s, axis=0, keepdims=True)
            p = jnp.exp(s - m)
            acc_ref[h] = jnp.dot(vt[h, qi], p.astype(BF16), preferred_element_type=F32)
            qs.append(q)
            ms.append(m)
            ls.append(jnp.sum(p, axis=0, keepdims=True))

        def past(it, st):
            ms, ls = st
            new_m, new_l = [], []
            for h in heads:
                m, l = ms[h], ls[h]
                blocks = [it * per_it + u for u in range(per_it)]
                ss = []
                for jb in blocks:
                    c0 = pl.multiple_of(jnp.minimum(jb, nb - 1) * blk, blk)
                    ss.append(lax.dot_general(k_ref[0, pl.ds(c0, blk), head_cols(h)], qs[h], _NT_DIMS,
                                              preferred_element_type=F32) * SCALE + sel_ref[h, pl.ds(jb, 1), :])
                m_new = m
                for s in ss:
                    m_new = jnp.maximum(m_new, jnp.max(s, axis=0, keepdims=True))
                alpha = jnp.exp(m - m_new)
                l = alpha * l
                acc = alpha * acc_ref[h]
                for jb, s in zip(blocks, ss):
                    p = jnp.exp(s - m_new)
                    l = l + jnp.sum(p, axis=0, keepdims=True)
                    acc = acc + jnp.dot(vt[h, jnp.minimum(jb, nb - 1)], p.astype(BF16),
                                        preferred_element_type=F32)
                acc_ref[h] = acc
                new_m.append(m_new)
                new_l.append(l)
            return tuple(new_m), tuple(new_l)

        ms, ls = lax.fori_loop(0, (qi + per_it - 1) // per_it, past, (tuple(ms), tuple(ls)))
        for h in heads:
            o_ref[0, pl.ds(r0, blk), head_cols(h)] = (
                (acc_ref[h] / jnp.maximum(ls[h], TINY)).T.astype(o_ref.dtype))
        return carry

    lax.fori_loop(0, nb, q_body, 0)


def _moba_attention(proj, batch, attn_w):
    t, e = proj.shape
    s = t // batch
    assert s % MOBA_BLOCK == 0
    nb = s // MOBA_BLOCK
    nbp = -(-nb // BF16_ROWS) * BF16_ROWS
    n_heads = attn_w // HEAD_DIM
    hp = MOBA_HEADS_PER_STEP
    assert n_heads % hp == 0
    w = hp * HEAD_DIM
    pv = proj.reshape(batch, s, e)

    def spec(c):
        return pl.BlockSpec((1, s, w), lambda b, h: (b, 0, c * (n_heads // hp) + h))

    out = pl.pallas_call(
        functools.partial(_moba_kernel, seq=s),
        grid=(batch, n_heads // hp),
        in_specs=[spec(0), spec(1), spec(2)],
        out_specs=pl.BlockSpec((1, s, w), lambda b, h: (b, 0, h)),
        out_shape=jax.ShapeDtypeStruct((batch, s, attn_w), BF16),
        scratch_shapes=[pltpu.VMEM((hp, nb, HEAD_DIM, MOBA_BLOCK), BF16),
                        pltpu.VMEM((hp, nbp, HEAD_DIM), F32),
                        pltpu.VMEM((hp, 3 * nbp, HEAD_DIM), BF16),
                        pltpu.VMEM((hp, nbp + MOBA_BLOCKS_PER_ITER, MOBA_BLOCK), F32),
                        pltpu.VMEM((hp, HEAD_DIM, MOBA_BLOCK), F32)],
        compiler_params=_params("parallel", "parallel"),
        name="moba_attn",
    )(pv, pv, pv)
    return out.reshape(t, attn_w)


def _router_kernel(x_ref, g_ref, wr_ref, i1_ref, i2_ref, g1_ref, g2_ref, *, n_experts):
    y = _rms(x_ref[...], g_ref[...])
    lg = jnp.dot(y, wr_ref[...], preferred_element_type=F32, precision=lax.Precision.HIGHEST)
    lane = lax.broadcasted_iota(jnp.int32, lg.shape, 1).astype(F32)
    lg = jnp.where(lane < n_experts, lg, NEG_INF)
    m1 = jnp.max(lg, axis=1, keepdims=True)
    i1 = jnp.min(jnp.where(lg == m1, lane, float(LANES)), axis=1, keepdims=True)
    lg2 = jnp.where(lane == i1, NEG_INF, lg)
    m2 = jnp.max(lg2, axis=1, keepdims=True)
    i2 = jnp.min(jnp.where(lg2 == m2, lane, float(LANES)), axis=1, keepdims=True)
    e = jnp.exp(m2 - m1)
    i1_ref[...] = i1.astype(jnp.int32)
    i2_ref[...] = i2.astype(jnp.int32)
    g1_ref[...] = 1.0 / (1.0 + e)
    g2_ref[...] = e / (1.0 + e)


def _router(h, g, w_router):
    t, d = h.shape
    n_experts = w_router.shape[1]
    assert n_experts <= LANES
    tm = 256
    wr = jnp.zeros((d, LANES), F32).at[:, :n_experts].set(w_router)
    col = pl.BlockSpec((tm, 1), lambda i: (i, 0))
    return pl.pallas_call(
        functools.partial(_router_kernel, n_experts=n_experts),
        grid=(t // tm,),
        in_specs=[pl.BlockSpec((tm, d), lambda i: (i, 0)), pl.BlockSpec((1, d), lambda i: (0, 0)),
                  pl.BlockSpec((d, LANES), lambda i: (0, 0))],
        out_specs=[col, col, col, col],
        out_shape=[jax.ShapeDtypeStruct((t, 1), jnp.int32)] * 2 + [jax.ShapeDtypeStruct((t, 1), F32)] * 2,
        compiler_params=_params("parallel"),
        name="router",
    )(h, g.reshape(1, d), wr)


def _route_plan(i1, i2, n_experts, tm):
    t = i1.shape[0]
    e = jnp.concatenate([i1, i2])
    onehot = (e[:, None] == jnp.arange(n_experts, dtype=jnp.int32)[None, :]).astype(jnp.int32)
    csum = jnp.cumsum(onehot, axis=0)
    rank = jnp.take_along_axis(csum, e[:, None], axis=1)[:, 0] - 1
    counts = csum[-1]
    padded = ((counts + tm - 1) // tm) * tm
    ends = jnp.cumsum(padded)
    starts = ends - padded
    pos = starts[e] + rank
    n_rows = TOP_K * t + n_experts * tm
    tok = jnp.arange(t, dtype=jnp.int32)
    row_token = jnp.zeros((n_rows,), jnp.int32).at[pos].set(jnp.concatenate([tok, tok]))
    tile_start = jnp.arange(n_rows // tm, dtype=jnp.int32) * tm
    grp = jnp.sum((tile_start[:, None] >= ends[None, :]).astype(jnp.int32), axis=1)
    in_use = grp < n_experts
    grp = jnp.minimum(grp, n_experts - 1)
    nrows = jnp.where(in_use, jnp.clip((starts + counts)[grp] - tile_start, 0, tm), 0).astype(jnp.int32)
    first = jnp.concatenate([jnp.ones((1,), jnp.int32), (grp[1:] != grp[:-1]).astype(jnp.int32)])
    g_start = jnp.arange(n_rows // GATHER_ROWS, dtype=jnp.int32) * GATHER_ROWS
    g_valid = ((g_start % tm) < nrows[g_start // tm]).astype(jnp.int32)
    return (grp, first, nrows), row_token, g_valid, pos[:t], pos[t:]


GATHER_ROWS = 256
DMA_ISSUE_UNROLL = 8


def _gather_norm_kernel(tok_ref, valid_ref, g_ref, h_hbm, o_ref, buf, sem, *, tm, n_steps):
    i = pl.program_id(0)
    slot = lax.rem(i, 2)

    def row_copy(step, r, s):
        return pltpu.make_async_copy(h_hbm.at[pl.ds(tok_ref[step * tm + r], 1)],
                                     buf.at[s, pl.ds(r, 1)], sem.at[s])

    def issue(step, s):
        def body(r, carry):
            row_copy(step, r, s).start()
            return carry
        lax.fori_loop(0, tm, body, 0, unroll=DMA_ISSUE_UNROLL)

    @pl.when(jnp.logical_and(i == 0, valid_ref[0] == 1))
    def _first():
        issue(0, 0)

    nxt = jnp.minimum(i + 1, n_steps - 1)

    @pl.when(jnp.logical_and(i + 1 < n_steps, valid_ref[nxt] == 1))
    def _prefetch():
        issue(nxt, 1 - slot)

    @pl.when(valid_ref[i] == 0)
    def _skip():
        o_ref[...] = jnp.zeros(o_ref.shape, o_ref.dtype)

    @pl.when(valid_ref[i] == 1)
    def _norm():
        def wait(r, carry):
            row_copy(i, r, slot).wait()
            return carry
        lax.fori_loop(0, tm, wait, 0, unroll=DMA_ISSUE_UNROLL)
        o_ref[...] = _rms(buf[slot], g_ref[...]).astype(o_ref.dtype)


def _gather_norm(h, g, row_token, valid, tm):
    t, d = h.shape
    n_rows = row_token.shape[0]
    grid_spec = pltpu.PrefetchScalarGridSpec(
        num_scalar_prefetch=2,
        grid=(n_rows // tm,),
        in_specs=[pl.BlockSpec((1, d), lambda i, tok, v: (0, 0)), pl.BlockSpec(memory_space=pl.ANY)],
        out_specs=pl.BlockSpec((tm, d), lambda i, tok, v: (i, 0)),
        scratch_shapes=[pltpu.VMEM((2, tm, d), F32), pltpu.SemaphoreType.DMA((2,))],
    )
    return pl.pallas_call(
        functools.partial(_gather_norm_kernel, tm=tm, n_steps=n_rows // tm),
        grid_spec=grid_spec,
        out_shape=jax.ShapeDtypeStruct((n_rows, d), BF16),
        compiler_params=_params("arbitrary"),
        name="moe_gather_norm",
    )(row_token, valid, g.reshape(1, d), h)


def _combine_kernel(p1_ref, p2_ref, h_ref, g1_ref, g2_ref, fn_ref, y_hbm, o_ref, buf, sem, *, tm, n_steps):
    i = pl.program_id(0)
    slot = lax.rem(i, 2)

    def row_copy(step, r, which, p_ref, s):
        return pltpu.make_async_copy(y_hbm.at[pl.ds(p_ref[step * tm + r], 1)],
                                     buf.at[s, which, pl.ds(r, 1)], sem.at[s])

    def issue(step, s):
        def body(r, carry):
            row_copy(step, r, 0, p1_ref, s).start()
            row_copy(step, r, 1, p2_ref, s).start()
            return carry
        lax.fori_loop(0, tm, body, 0, unroll=DMA_ISSUE_UNROLL)

    @pl.when(i == 0)
    def _first():
        issue(0, 0)

    @pl.when(i + 1 < n_steps)
    def _prefetch():
        issue(i + 1, 1 - slot)

    def wait(r, carry):
        row_copy(i, r, 0, p1_ref, slot).wait()
        row_copy(i, r, 1, p2_ref, slot).wait()
        return carry
    lax.fori_loop(0, tm, wait, 0, unroll=DMA_ISSUE_UNROLL)
    h = h_ref[...] + (g1_ref[...] * buf[slot, 0] + g2_ref[...] * buf[slot, 1])
    o_ref[...] = _rms(h, fn_ref[...])


def _combine_norm(h, y, pos1, pos2, g1, g2, final_norm):
    t, d = h.shape
    tm = 256
    col = pl.BlockSpec((tm, 1), lambda i, a, b: (i, 0))
    row = pl.BlockSpec((tm, d), lambda i, a, b: (i, 0))
    grid_spec = pltpu.PrefetchScalarGridSpec(
        num_scalar_prefetch=2,
        grid=(t // tm,),
        in_specs=[row, col, col, pl.BlockSpec((1, d), lambda i, a, b: (0, 0)),
                  pl.BlockSpec(memory_space=pl.ANY)],
        out_specs=row,
        scratch_shapes=[pltpu.VMEM((2, TOP_K, tm, d), F32), pltpu.SemaphoreType.DMA((2,))],
    )
    return pl.pallas_call(
        functools.partial(_combine_kernel, tm=tm, n_steps=t // tm),
        grid_spec=grid_spec,
        out_shape=jax.ShapeDtypeStruct((t, d), F32),
        compiler_params=_params("arbitrary"),
        name="moe_combine_norm",
    )(pos1, pos2, h, g1, g2, final_norm.reshape(1, d), y)


TM_DENSE = 1024
TN_DENSE = 512
TN_STREAM_ROPE = 1024
TN_STREAM_SWIGLU = 512
TM_EXPERT = 512
TN_SWIGLU_EXPERT = 512
TN_DOWN_EXPERT = 1024
DOWN_K_SPLITS = 4


def _down_proj(a, w, plan, res, tm, tn):
    k = w.shape[1]
    splits = DOWN_K_SPLITS if k > 8192 else 1
    kb = k // splits
    for ki in range(splits):
        res = _matmul(a, (w,), plan, mode="residual", tm=tm, tn=tn, out_dtype=F32,
                      k_blk=kb, k_idx=ki, extras=(res,))
    return res


def kernel(x, positions, mix_norm, ffn_norm, dil_w_in, dil_w_out, moba_w_in, moba_w_out,
           ffn_w_gate, ffn_w_up, ffn_w_down, router_w, exp_w_gate, exp_w_up, exp_w_down,
           final_norm):
    batch, s, d = x.shape
    t = batch * s
    attn_w = dil_w_out.shape[1]
    h = x.reshape(t, d)
    rope = _rope_tables(positions)
    dense = _dense_plan(t // TM_DENSE, TM_DENSE)

    hn = _rmsnorm(h, mix_norm[0])
    proj = _matmul_stream(hn, (dil_w_in,), mode="rope", tm=TM_DENSE, tn=TN_STREAM_ROPE, out_dtype=BF16,
                          extras=rope, qkv_width=attn_w)
    o = _dilated_attention(proj, batch, attn_w)
    h = _matmul(o, (dil_w_out,), dense, mode="residual", tm=TM_DENSE, tn=TN_DENSE, out_dtype=F32,
                extras=(h,))
    hn = _rmsnorm(h, ffn_norm[0])
    a = _matmul_stream(hn, (ffn_w_gate, ffn_w_up), mode="swiglu", tm=TM_DENSE, tn=TN_STREAM_SWIGLU,
                       out_dtype=BF16)
    h = _down_proj(a, ffn_w_down, dense, h, TM_DENSE, TN_DENSE)

    hn = _rmsnorm(h, mix_norm[1])
    proj = _matmul_stream(hn, (moba_w_in,), mode="rope", tm=TM_DENSE, tn=TN_STREAM_ROPE, out_dtype=BF16,
                          extras=rope, qkv_width=attn_w)
    o = _moba_attention(proj, batch, attn_w)
    h = _matmul(o, (moba_w_out,), dense, mode="residual", tm=TM_DENSE, tn=TN_DENSE, out_dtype=F32,
                extras=(h,))

    n_experts = router_w.shape[-1]
    i1, i2, g1, g2 = _router(h, ffn_norm[1], router_w[0])
    plan, row_token, g_valid, pos1, pos2 = _route_plan(i1[:, 0], i2[:, 0], n_experts, TM_EXPERT)
    xs = _gather_norm(h, ffn_norm[1], row_token, g_valid, GATHER_ROWS)
    a = _matmul(xs, (exp_w_gate[0], exp_w_up[0]), plan, mode="swiglu", tm=TM_EXPERT,
                tn=TN_SWIGLU_EXPERT, out_dtype=BF16)
    y = _matmul(a, (exp_w_down[0],), plan, mode="plain", tm=TM_EXPERT, tn=TN_DOWN_EXPERT, out_dtype=F32)
    out = _combine_norm(h, y, pos1, pos2, g1, g2, final_norm)
    return out.reshape(batch, s, d)
```
